```python
import math, functools
import jax, jax.numpy as jnp
from jax import lax
import numpy as np

D_MODEL = 1024
BATCH = 4
SEQ = 8192
DEPTH = 1
DEC_BATCH = 128
DEC_SEQ = 4
PAST_LEN = 8192
PAGE_SIZE = 128

DN_HEADS = 4
DN_DK = 128
DN_DV = 128
DN_CONV = 4
DN_CHUNK = 64
DN_QK = DN_HEADS * DN_DK
DN_V = DN_HEADS * DN_DV
DN_CONV_CH = 2 * DN_QK + DN_V
DA_HEADS = 4
DA_DQK = 64
DA_DV = 128
DA_QK = DA_HEADS * 2 * DA_DQK
DA_V = DA_HEADS * DA_DV
Q_BLOCK = 128
ROPE_THETA = 10000.0
MIX_WIDTH = DN_V + DA_V
IN_DIM = DN_CONV_CH + DN_V + 2 * DN_HEADS + 2 * DA_QK + DA_V
PEER_HEADS = 8
PEER_DK = 256
N_KEYS = 128
N_EXPERTS = N_KEYS * N_KEYS
PEER_TOPK = 16
PEER_BLOCK = 128
DEEPNORM_ALPHA = (2 * DEPTH) ** 0.25
DEEPNORM_BETA = (8 * DEPTH) ** -0.25

kernel_name = 'hybrid_gdn_diffattn_peer_step'


def layer_norm(x, g, b, eps=1e-5):
    xf = x.astype(jnp.float32)
    mu = jnp.mean(xf, -1, keepdims=True)
    var = jnp.mean(jnp.square(xf - mu), -1, keepdims=True)
    return ((xf - mu) * lax.rsqrt(var + eps) * g.astype(jnp.float32) + b.astype(jnp.float32)).astype(x.dtype)


def rms_norm(x, w, eps=1e-6):
    xf = x.astype(jnp.float32)
    return xf * lax.rsqrt(jnp.mean(xf * xf, -1, keepdims=True) + eps) * w.astype(jnp.float32)


def l2_normalize(x, eps=1e-6):
    xf = x.astype(jnp.float32)
    return xf * lax.rsqrt(jnp.sum(xf * xf, -1, keepdims=True) + eps)


def rope(x, pos):
    half = DA_DQK // 2
    inv_freq = ROPE_THETA ** (-jnp.arange(half, dtype=jnp.float32) * 2.0 / DA_DQK)
    ang = pos.astype(jnp.float32)[:, None] * inv_freq[None, :]
    cos = jnp.cos(ang)[:, None, None, :]
    sin = jnp.sin(ang)[:, None, None, :]
    xf = x.astype(jnp.float32)
    x1, x2 = xf[..., :half], xf[..., half:]
    return jnp.concatenate([x1 * cos - x2 * sin, x2 * cos + x1 * sin], -1).astype(x.dtype)


def short_conv(qkv, conv_buf, conv_w):
    T = qkv.shape[1]
    xc = jnp.concatenate([conv_buf.astype(qkv.dtype), qkv], axis=1)
    acc = xc[:, 0:T] * conv_w[0]
    for w in range(1, DN_CONV):
        acc = acc + xc[:, w:w + T] * conv_w[w]
    return jax.nn.silu(acc), xc[:, T:]


def to_chunks(t, n, C):
    pad = n * C - t.shape[1]
    t = jnp.pad(t, [(0, 0), (0, pad)] + [(0, 0)] * (t.ndim - 2))
    t = t.reshape((t.shape[0], n, C) + t.shape[2:])
    perm = (1, 0, 3, 2, 4) if t.ndim == 5 else (1, 0, 3, 2)
    return t.transpose(perm)


def gated_delta_rule(q, k, v, g, beta, s0):
    B, T, H, _ = q.shape
    DV = v.shape[-1]
    C = min(DN_CHUNK, T)
    n = -(-T // C)
    qc, kc, vc = to_chunks(q, n, C), to_chunks(k, n, C), to_chunks(v.astype(jnp.float32), n, C)
    gc, bc = to_chunks(g, n, C), to_chunks(beta, n, C)
    G = jnp.cumsum(gc, axis=-1)
    idx = jnp.arange(C)
    incl = idx[:, None] >= idx[None, :]
    strict = idx[:, None] > idx[None, :]
    L = jnp.exp(jnp.where(incl, G[..., :, None] - G[..., None, :], -jnp.inf))
    kb = kc * bc[..., None]
    vb = vc * bc[..., None]
    M = jnp.eye(C, dtype=jnp.float32) + jnp.where(strict, jnp.einsum('nbhid,nbhjd->nbhij', kb, kc) * L, 0.0)
    U = lax.linalg.triangular_solve(M, vb, left_side=True, lower=True, unit_diagonal=True)
    W = lax.linalg.triangular_solve(M, kb * jnp.exp(G)[..., None], left_side=True, lower=True, unit_diagonal=True)
    Aqk = jnp.einsum('nbhid,nbhjd->nbhij', qc, kc) * L
    qg = qc * jnp.exp(G)[..., None]
    kd = kc * jnp.exp(G[..., -1:] - G)[..., None]
    glast = jnp.exp(G[..., -1])

    def step(S, xs):
        U_, W_, A_, qg_, kd_, gl_ = xs
        v_new = U_ - jnp.einsum('bhcd,bhde->bhce', W_, S)
        o = jnp.einsum('bhcd,bhde->bhce', qg_, S) + jnp.einsum('bhij,bhje->bhie', A_, v_new)
        S = S * gl_[..., None, None] + jnp.einsum('bhcd,bhce->bhde', kd_, v_new)
        return S, o

    S, o = lax.scan(step, s0.astype(jnp.float32), (U, W, Aqk, qg, kd, glast))
    o = o.transpose(1, 0, 3, 2, 4).reshape(B, n * C, H, DV)[:, :T]
    return o, S


def deltanet_group(qkv, z, a, b, conv_buf, s0, conv_w, a_log, dt_bias, norm_w):
    B, T, _ = qkv.shape
    act, new_buf = short_conv(qkv, conv_buf, conv_w)
    q = l2_normalize(act[..., :DN_QK].reshape(B, T, DN_HEADS, DN_DK)) * (DN_DK ** -0.5)
    k = l2_normalize(act[..., DN_QK:2 * DN_QK].reshape(B, T, DN_HEADS, DN_DK))
    v = act[..., 2 * DN_QK:].reshape(B, T, DN_HEADS, DN_DV)
    beta = jax.nn.sigmoid(b.astype(jnp.float32))
    g = -jnp.exp(a_log.astype(jnp.float32)) * jax.nn.softplus(a.astype(jnp.float32) + dt_bias.astype(jnp.float32))
    o, s_new = gated_delta_rule(q, k, v, g, beta, s0)
    o = rms_norm(o, norm_w) * jax.nn.silu(z.reshape(B, T, DN_HEADS, DN_DV).astype(jnp.float32))
    return o.reshape(B, T, DN_V), s_new, new_buf


def prompt_diff_attention(q, k, v, lam):
    B, T = q.shape[0], q.shape[1]
    nqb = T // Q_BLOCK
    scale = DA_DQK ** -0.5
    qb = q.reshape(B, nqb, Q_BLOCK, DA_HEADS, 2, DA_DQK).transpose(1, 0, 2, 3, 4, 5)
    starts = jnp.arange(nqb) * Q_BLOCK
    kpos = jnp.arange(T)
    vf = v.astype(jnp.float32)

    def block(args):
        qblk, start = args
        s = jnp.einsum('bqhcd,bkhcd->bhcqk', qblk, k).astype(jnp.float32) * scale
        qpos = start + jnp.arange(Q_BLOCK)
        s = jnp.where(kpos[None, :] <= qpos[:, None], s, -jnp.inf)
        p = jax.nn.softmax(s, axis=-1)
        pd = p[:, :, 0] - lam * p[:, :, 1]
        return jnp.einsum('bhqk,bkhd->bqhd', pd, vf)

    o = lax.map(block, (qb, starts))
    return o.transpose(1, 0, 2, 3, 4).reshape(B, T, DA_HEADS, DA_DV)


def paged_diff_attention(cache_k, cache_v, page_table, layer):
    def attend(q, k, v, lam):
        S = q.shape[1]
        causal = jnp.tril(jnp.ones((S, S), dtype=bool))
        scale = DA_DQK ** -0.5

        def one_seq(args):
            qs, ks, vs, pages = args
            kp = cache_k[layer, pages].reshape(-1, DA_HEADS, 2, DA_DQK)
            vp = cache_v[layer, pages].reshape(-1, DA_HEADS, DA_DV)
            s_past = jnp.einsum('qhcd,khcd->hcqk', qs, kp).astype(jnp.float32) * scale
            s_new = jnp.einsum('qhcd,khcd->hcqk', qs, ks).astype(jnp.float32) * scale
            s_new = jnp.where(causal, s_new, -jnp.inf)
            p = jax.nn.softmax(jnp.concatenate([s_past, s_new], axis=-1), axis=-1)
            pd = p[:, 0] - lam * p[:, 1]
            n_past = kp.shape[0]
            return (jnp.einsum('hqk,khd->qhd', pd[..., :n_past], vp.astype(jnp.float32))
                    + jnp.einsum('hqk,khd->qhd', pd[..., n_past:], vs.astype(jnp.float32)))

        return lax.map(one_seq, (q, k, v, page_table))
    return attend


def peer_block(xb, wq, sub_keys, u_tab, v_tab):
    n = xb.shape[0]
    q = (xb @ wq).reshape(n, PEER_HEADS, 2, PEER_DK // 2)
    s = jnp.einsum('nhcd,ckd->nhck', q, sub_keys).astype(jnp.float32)
    s1, i1 = lax.top_k(s[:, :, 0], PEER_TOPK)
    s2, i2 = lax.top_k(s[:, :, 1], PEER_TOPK)
    cand = (s1[..., :, None] + s2[..., None, :]).reshape(n, PEER_HEADS, PEER_TOPK * PEER_TOPK)
    cidx = (i1[..., :, None] * N_KEYS + i2[..., None, :]).reshape(n, PEER_HEADS, PEER_TOPK * PEER_TOPK)
    top, j = lax.top_k(cand, PEER_TOPK)
    eidx = jnp.take_along_axis(cidx, j, axis=-1)
    gate = jax.nn.softmax(top, axis=-1)
    hact = jax.nn.gelu(jnp.einsum('nd,nhkd->nhk', xb, u_tab[eidx]).astype(jnp.float32), approximate=False)
    return jnp.einsum('nhk,nhkd->nd', (gate * hact).astype(xb.dtype), v_tab[eidx])


def peer_ffn(x2d, wq, sub_keys, u_tab, v_tab):
    n = x2d.shape[0]
    blk = min(PEER_BLOCK, n)
    nb = -(-n // blk)
    xb = jnp.pad(x2d, ((0, nb * blk - n), (0, 0))).reshape(nb, blk, D_MODEL)
    out = lax.map(lambda xx: peer_block(xx, wq, sub_keys, u_tab, v_tab), xb)
    return out.reshape(nb * blk, D_MODEL)[:n]


def trunk_layer(x, pos, conv_buf, dn_state, attend_fn, lam, lam_init, w_in, dn_conv_w, dn_a_log, dn_dt_bias,
                dn_norm_w, da_norm_w, w_out, ln1_g, ln1_b, peer_wq, peer_sub_keys, peer_u, peer_v, ln2_g, ln2_b):
    B, T, _ = x.shape
    proj = x @ w_in
    o = 0
    qkv = proj[..., o:o + DN_CONV_CH]; o += DN_CONV_CH
    z = proj[..., o:o + DN_V]; o += DN_V
    a = proj[..., o:o + DN_HEADS]; o += DN_HEADS
    b = proj[..., o:o + DN_HEADS]; o += DN_HEADS
    daq = proj[..., o:o + DA_QK]; o += DA_QK
    dak = proj[..., o:o + DA_QK]; o += DA_QK
    dav = proj[..., o:o + DA_V]
    dn_out, s_new, buf_new = deltanet_group(qkv, z, a, b, conv_buf, dn_state, dn_conv_w, dn_a_log, dn_dt_bias, dn_norm_w)
    q = rope(daq.reshape(B, T, DA_HEADS, 2, DA_DQK), pos)
    k = rope(dak.reshape(B, T, DA_HEADS, 2, DA_DQK), pos)
    v = dav.reshape(B, T, DA_HEADS, DA_DV)
    da = attend_fn(q, k, v, lam)
    da = rms_norm(da, da_norm_w) * (1.0 - lam_init)
    mix = jnp.concatenate([dn_out, da.reshape(B, T, DA_V)], axis=-1).astype(x.dtype) @ w_out
    x1 = layer_norm(DEEPNORM_ALPHA * x + mix, ln1_g, ln1_b)
    ffn = peer_ffn(x1.reshape(B * T, D_MODEL), peer_wq, peer_sub_keys, peer_u, peer_v).reshape(B, T, D_MODEL)
    x2 = layer_norm(DEEPNORM_ALPHA * x1 + ffn, ln2_g, ln2_b)
    return x2, k.reshape(B, T, DA_HEADS, 2 * DA_DQK), v, s_new, buf_new


def setup_inputs(seed: int = 0) -> dict:
    key = jax.random.key(seed)
    ks = jax.random.split(key, 32)
    f32 = jnp.float32
    n_pages = PAST_LEN // PAGE_SIZE
    n_used = DEC_BATCH * n_pages
    n_phys = n_used + max(1, n_used // 4)

    def nrm(k, shape, s):
        return jax.random.normal(k, shape, f32) * s

    dt = jnp.exp(jax.random.uniform(ks[10], (DEPTH, DN_HEADS), f32, math.log(1e-3), math.log(1e-1)))
    return {
        'x_prompt': nrm(ks[0], (BATCH, SEQ, D_MODEL), 1.0),
        'x_sample': nrm(ks[1], (DEC_BATCH, DEC_SEQ, D_MODEL), 1.0),
        'cache_k': nrm(ks[2], (DEPTH, n_phys, PAGE_SIZE, DA_HEADS, 2 * DA_DQK), 1.0),
        'cache_v': nrm(ks[3], (DEPTH, n_phys, PAGE_SIZE, DA_HEADS, DA_DV), 1.0),
        'state_dn': nrm(ks[4], (DEPTH, DEC_BATCH, DN_HEADS, DN_DK, DN_DV), 0.1),
        'state_conv': nrm(ks[5], (DEPTH, DEC_BATCH, DN_CONV - 1, DN_CONV_CH), 1.0),
        'page_table': jax.random.permutation(ks[6], n_phys)[:n_used].reshape(DEC_BATCH, n_pages).astype(jnp.int32),
        'w_in': nrm(ks[7], (DEPTH, D_MODEL, IN_DIM), D_MODEL ** -0.5),
        'dn_conv_w': nrm(ks[8], (DEPTH, DN_CONV, DN_CONV_CH), DN_CONV ** -0.5),
        'dn_a_log': jnp.log(jax.random.uniform(ks[9], (DEPTH, DN_HEADS), f32, 1.0, 16.0)),
        'dn_dt_bias': dt + jnp.log(-jnp.expm1(-dt)),
        'dn_norm_w': 1.0 + nrm(ks[11], (DEPTH, DN_DV), 0.02),
        'da_lambda_q1': nrm(ks[12], (DEPTH, DA_DQK), 0.1),
        'da_lambda_k1': nrm(ks[13], (DEPTH, DA_DQK), 0.1),
        'da_lambda_q2': nrm(ks[14], (DEPTH, DA_DQK), 0.1),
        'da_lambda_k2': nrm(ks[15], (DEPTH, DA_DQK), 0.1),
        'da_norm_w': 1.0 + nrm(ks[16], (DEPTH, DA_DV), 0.02),
        'w_out': nrm(ks[17], (DEPTH, MIX_WIDTH, D_MODEL), MIX_WIDTH ** -0.5 * DEEPNORM_BETA),
        'ln1_g': 1.0 + nrm(ks[18], (DEPTH, D_MODEL), 0.02),
        'ln1_b': nrm(ks[19], (DEPTH, D_MODEL), 0.02),
        'peer_wq': nrm(ks[20], (DEPTH, D_MODEL, PEER_HEADS * PEER_DK), D_MODEL ** -0.5),
        'peer_sub_keys': nrm(ks[21], (DEPTH, 2, N_KEYS, PEER_DK // 2), (PEER_DK // 2) ** -0.5),
        'peer_u': nrm(ks[22], (DEPTH, N_EXPERTS, D_MODEL), D_MODEL ** -0.5),
        'peer_v': nrm(ks[23], (DEPTH, N_EXPERTS, D_MODEL), DEEPNORM_BETA),
        'ln2_g': 1.0 + nrm(ks[24], (DEPTH, D_MODEL), 0.02),
        'ln2_b': nrm(ks[25], (DEPTH, D_MODEL), 0.02),
    }


def reference(x_prompt, x_sample, cache_k, cache_v, state_dn, state_conv, page_table, w_in, dn_conv_w, dn_a_log,
              dn_dt_bias, dn_norm_w, da_lambda_q1, da_lambda_k1, da_lambda_q2, da_lambda_k2, da_norm_w, w_out,
              ln1_g, ln1_b, peer_wq, peer_sub_keys, peer_u, peer_v, ln2_g, ln2_b):
    f32 = jnp.float32
    B, T, _ = x_prompt.shape
    S = x_sample.shape[1]
    past_len = page_table.shape[1] * cache_k.shape[2]
    pos_prompt = jnp.arange(T)
    pos_sample = past_len + jnp.arange(S)
    h_p, h_s = x_prompt, x_sample
    kp_l, vp_l, ks_l, vs_l, sp_l, cp_l, ss_l, cs_l = [], [], [], [], [], [], [], []
    for l in range(DEPTH):
        lam_init = 0.8 - 0.6 * math.exp(-0.3 * l)
        lam = (jnp.exp(jnp.sum(da_lambda_q1[l].astype(f32) * da_lambda_k1[l].astype(f32)))
               - jnp.exp(jnp.sum(da_lambda_q2[l].astype(f32) * da_lambda_k2[l].astype(f32))) + lam_init)
        layer_fn = functools.partial(
            trunk_layer, lam=lam, lam_init=lam_init, w_in=w_in[l], dn_conv_w=dn_conv_w[l], dn_a_log=dn_a_log[l],
            dn_dt_bias=dn_dt_bias[l], dn_norm_w=dn_norm_w[l], da_norm_w=da_norm_w[l], w_out=w_out[l],
            ln1_g=ln1_g[l], ln1_b=ln1_b[l], peer_wq=peer_wq[l], peer_sub_keys=peer_sub_keys[l],
            peer_u=peer_u[l], peer_v=peer_v[l], ln2_g=ln2_g[l], ln2_b=ln2_b[l])
        h_p, k_p, v_p, s_p, c_p = layer_fn(h_p, pos_prompt,
                                           jnp.zeros((B, DN_CONV - 1, DN_CONV_CH), x_prompt.dtype),
                                           jnp.zeros((B, DN_HEADS, DN_DK, DN_DV), f32),
                                           prompt_diff_attention)
        h_s, k_s, v_s, s_s, c_s = layer_fn(h_s, pos_sample, state_conv[l], state_dn[l],
                                           paged_diff_attention(cache_k, cache_v, page_table, l))
        kp_l.append(k_p); vp_l.append(v_p); ks_l.append(k_s); vs_l.append(v_s)
        sp_l.append(s_p); cp_l.append(c_p); ss_l.append(s_s); cs_l.append(c_s)
    return (h_p, h_s, jnp.stack(kp_l), jnp.stack(vp_l), jnp.stack(ks_l), jnp.stack(vs_l),
            jnp.stack(sp_l), jnp.stack(cp_l), jnp.stack(ss_l), jnp.stack(cs_l))
```

```python
import functools
import math

import jax
import jax.numpy as jnp
from jax import lax
from jax.experimental import pallas as pl
from jax.experimental.pallas import tpu as pltpu

F32 = jnp.float32
BF16 = jnp.bfloat16
I32 = jnp.int32
HI = lax.Precision.HIGHEST

LANES = 128
SUBLANES = 8
VMEM_PHYSICAL = 64 * 1024 * 1024

D_MODEL = 1024
DN_HEADS = 4
DN_D = 128
DN_CONV = 4
DN_QK = DN_HEADS * DN_D
DN_CH = 3 * DN_QK
DN_CHUNK = 64
DA_HEADS = 4
DA_DQK = 64
DA_DV = 128
DA_W = DA_HEADS * DA_DV
ROPE_THETA = 10000.0
PEER_HEADS = 8
PEER_DK = 256
N_KEYS = 128
PEER_TOPK = 16
N_PAIRS = PEER_HEADS * PEER_TOPK
HALF_EXPERTS = N_KEYS * N_KEYS // 2
PROJ_W = DN_CH + DN_QK + 3 * DA_W + LANES
NEG = -1e30


def _cparams(sem, vmem_bytes):
    return pltpu.CompilerParams(dimension_semantics=sem, vmem_limit_bytes=int(vmem_bytes))


def _row_tile(m, cap):
    t = cap
    while m % t:
        t //= 2
    assert t >= SUBLANES, (m, cap)
    return t


def _rope_cols(t, cos, sin_signed):
    lane = lax.broadcasted_iota(I32, (t.shape[0], LANES), 1)
    first = (lane & (DA_DQK - 1)) < (DA_DQK // 2)
    outs = []
    for h in range(DA_HEADS):
        th = t[:, h * LANES:(h + 1) * LANES]
        partner = jnp.where(first, pltpu.roll(th, LANES - DA_DQK // 2, axis=1), pltpu.roll(th, DA_DQK // 2, axis=1))
        outs.append(th * cos + partner * sin_signed)
    return jnp.concatenate(outs, axis=1)


def _proj_kernel(x_ref, w_ref, cos_ref, sin_ref, qkv_ref, z_ref, ab_ref, q_ref, kf_ref, kb_ref, vf_ref, vb_ref):
    xb = x_ref[...].astype(BF16)

    def mm(lo, hi):
        return jnp.dot(xb, w_ref[:, lo:hi], preferred_element_type=F32)

    qkv_ref[...] = mm(0, DN_CH)
    o = DN_CH
    z_ref[...] = mm(o, o + DN_QK)
    o += DN_QK
    cos = cos_ref[...]
    sin = sin_ref[...]
    q = _rope_cols(mm(o, o + DA_W), cos, sin)
    q_ref[...] = (q * (DA_DQK ** -0.5)).astype(BF16)
    o += DA_W
    k = _rope_cols(mm(o, o + DA_W), cos, sin)
    kf_ref[...] = k
    kb_ref[...] = k.astype(BF16)
    o += DA_W
    v = mm(o, o + DA_W)
    vf_ref[...] = v
    vb_ref[...] = v.astype(BF16)
    o += DA_W
    ab_ref[...] = mm(o, o + LANES)


def _proj(x2d, w_r, cos_t, sin_t):
    m = x2d.shape[0]
    tm = _row_tile(m, 256)
    row = lambda w: pl.BlockSpec((tm, w), lambda i: (i, 0))
    outs = [(DN_CH, F32), (DN_QK, F32), (LANES, F32), (DA_W, BF16), (DA_W, F32), (DA_W, BF16), (DA_W, F32), (DA_W, BF16)]
    return pl.pallas_call(
        _proj_kernel,
        grid=(m // tm,),
        in_specs=[row(D_MODEL), pl.BlockSpec((D_MODEL, PROJ_W), lambda i: (0, 0)), row(LANES), row(LANES)],
        out_specs=[row(w) for w, _ in outs],
        out_shape=[jax.ShapeDtypeStruct((m, w), d) for w, d in outs],
        compiler_params=_cparams(("parallel",), 48 << 20),
        name="proj",
    )(x2d, w_r, cos_t, sin_t)


def _dot(a, b, prec=HI):
    return jnp.dot(a, b, precision=prec, preferred_element_type=F32)


def _dot_nt(a, b, prec=HI):
    return lax.dot_general(a, b, (((1,), (1,)), ((), ())), precision=prec, preferred_element_type=F32)


def _dot_tn(a, b, prec=HI):
    return lax.dot_general(a, b, (((0,), (0,)), ((), ())), precision=prec, preferred_element_type=F32)


def _unit_lower_inverse(a, c):
    r = lax.broadcasted_iota(I32, (c, c), 0)
    s = lax.broadcasted_iota(I32, (c, c), 1)
    base = min(c, 16)
    lb = base.bit_length() - 1
    n = jnp.where((r >> lb) == (s >> lb), -a, 0.0)
    p = jnp.where(r == s, 1.0, 0.0) + n
    npow = n
    for _ in range(lb - 1):
        npow = _dot(npow, npow)
        p = p + _dot(p, npow)
    while base < c:
        lb = base.bit_length() - 1
        e_mask = ((r >> (lb + 1)) == (s >> (lb + 1))) & (((r >> lb) & 1) == 1) & (((s >> lb) & 1) == 0)
        e = jnp.where(e_mask, a, 0.0)
        p = p - _dot(_dot(p, e), p)
        base *= 2
    return p


def _silu(x):
    return x * (1.0 / (1.0 + jnp.exp(-x)))


def _dn_kernel(c, nvalid, has_state, *refs):
    if has_state:
        (qkv_ref, z_ref, ab_ref, cw_ref, hp_ref, nw_ref, s0_ref, cb0_ref,
         out_ref, sfin_ref, cfin_ref, s_scr, xbuf) = refs
    else:
        (qkv_ref, z_ref, ab_ref, cw_ref, hp_ref, nw_ref,
         out_ref, sfin_ref, cfin_ref, s_scr, xbuf) = refs
    i = pl.program_id(1)
    last = pl.num_programs(1) - 1
    base = SUBLANES
    hist = DN_CONV - 1

    @pl.when(i == 0)
    def _():
        if has_state:
            s_scr[...] = s0_ref[0]
            xbuf[base - hist:base, :] = cb0_ref[0]
        else:
            s_scr[...] = jnp.zeros_like(s_scr)
            xbuf[0:base, :] = jnp.zeros((base, DN_CH), F32)

    xbuf[base:base + c, :] = qkv_ref[...]
    acc = xbuf[base - hist:base - hist + c, :] * cw_ref[0:1, :]
    for w in range(1, DN_CONV):
        acc = acc + xbuf[base - hist + w:base - hist + w + c, :] * cw_ref[w:w + 1, :]
    act = _silu(acc)

    @pl.when(i == last)
    def _():
        cfin_ref[0] = xbuf[base + nvalid - hist:base + nvalid, :]

    xbuf[base - hist:base, :] = xbuf[base + c - hist:base + c, :]

    ab = ab_ref[...]
    rowi = lax.broadcasted_iota(I32, (c, LANES), 0)
    lane = lax.broadcasted_iota(I32, (c, LANES), 1)
    live = rowi < nvalid
    sp_in = ab + hp_ref[1:2, :]
    softplus = jnp.maximum(sp_in, 0.0) + jnp.log(1.0 + jnp.exp(-jnp.abs(sp_in)))
    g_all = jnp.where(live, -jnp.exp(hp_ref[0:1, :]) * softplus, 0.0)
    beta_all = jnp.where(live, 1.0 / (1.0 + jnp.exp(-ab)), 0.0)
    rr = lax.broadcasted_iota(I32, (c, c), 0)
    cc = lax.broadcasted_iota(I32, (c, c), 1)
    incl = rr >= cc
    strict = rr > cc
    gcum = _dot(jnp.where(incl, 1.0, 0.0), g_all)
    ones_cl = jnp.ones((c, LANES), F32)

    for h in range(DN_HEADS):
        qh = act[:, h * DN_D:(h + 1) * DN_D]
        kh = act[:, DN_QK + h * DN_D:DN_QK + (h + 1) * DN_D]
        vh = act[:, 2 * DN_QK + h * DN_D:2 * DN_QK + (h + 1) * DN_D]
        qh = qh * lax.rsqrt(jnp.sum(qh * qh, axis=-1, keepdims=True) + 1e-6) * (DN_D ** -0.5)
        kh = kh * lax.rsqrt(jnp.sum(kh * kh, axis=-1, keepdims=True) + 1e-6)
        beta = beta_all[:, DN_HEADS + h:DN_HEADS + h + 1]
        gcol = gcum[:, h:h + 1]
        grow = _dot_nt(ones_cl, jnp.where(lane == h, gcum, 0.0))
        decay = jnp.exp(jnp.where(incl, gcol - grow, NEG))
        eg = jnp.exp(gcol)
        glast = gcum[c - 1:c, h:h + 1]
        kb = kh * beta
        vb = vh * beta
        a = jnp.where(strict, _dot_nt(kb, kh) * decay, 0.0)
        tinv = _unit_lower_inverse(a, c)
        u = _dot(tinv, vb)
        w = _dot(tinv, kb * eg)
        aqk = _dot_nt(qh, kh) * decay
        s = s_scr[h]
        v_new = u - _dot(w, s)
        o = _dot(qh * eg, s) + _dot(aqk, v_new)
        s_new = s * jnp.exp(glast) + _dot_tn(kh * jnp.exp(glast - gcol), v_new)
        s_scr[h] = s_new
        zh = z_ref[:, h * DN_D:(h + 1) * DN_D]
        on = o * lax.rsqrt(jnp.mean(o * o, axis=-1, keepdims=True) + 1e-6) * nw_ref[...]
        out_ref[:, h * DN_D:(h + 1) * DN_D] = (on * _silu(zh)).astype(out_ref.dtype)

    @pl.when(i == last)
    def _():
        sfin_ref[0] = s_scr[...]


def _deltanet(qkv, z, ab, conv_w, hp, norm_w, nb, t_rows, c, nvalid, s0=None, cb0=None):
    nc = t_rows // c
    has_state = s0 is not None
    row = lambda w: pl.BlockSpec((c, w), lambda b, i: (b * nc + i, 0))
    full = lambda shape: pl.BlockSpec(shape, lambda b, i: (0,) * len(shape))
    in_specs = [row(DN_CH), row(DN_QK), row(LANES), full((DN_CONV, DN_CH)), full((SUBLANES, LANES)), full((1, DN_D))]
    args = [qkv, z, ab, conv_w, hp, norm_w]
    if has_state:
        in_specs += [pl.BlockSpec((1, DN_HEADS, DN_D, DN_D), lambda b, i: (b, 0, 0, 0)),
                     pl.BlockSpec((1, DN_CONV - 1, DN_CH), lambda b, i: (b, 0, 0))]
        args += [s0, cb0]
    return pl.pallas_call(
        functools.partial(_dn_kernel, c, nvalid, has_state),
        grid=(nb, nc),
        in_specs=in_specs,
        out_specs=[row(DN_QK),
                   pl.BlockSpec((1, DN_HEADS, DN_D, DN_D), lambda b, i: (b, 0, 0, 0)),
                   pl.BlockSpec((1, DN_CONV - 1, DN_CH), lambda b, i: (b, 0, 0))],
        out_shape=[jax.ShapeDtypeStruct((nb * t_rows, DN_QK), BF16),
                   jax.ShapeDtypeStruct((nb, DN_HEADS, DN_D, DN_D), F32),
                   jax.ShapeDtypeStruct((nb, DN_CONV - 1, DN_CH), F32)],
        scratch_shapes=[pltpu.VMEM((DN_HEADS, DN_D, DN_D), F32), pltpu.VMEM((c + SUBLANES, DN_CH), F32)],
        compiler_params=_cparams(("parallel", "arbitrary"), 32 << 20),
        name="deltanet",
    )(*args)


def _da_finish(o1, o2, lam, nw, post):
    o = o1 - lam * o2
    return o * lax.rsqrt(jnp.mean(o * o, axis=-1, keepdims=True) + 1e-6) * nw * post


def _attn_kernel(tq, post, lam_ref, q_ref, k_ref, v_ref, nw_ref, out_ref, m_scr, l_scr, acc_scr):
    i = pl.program_id(2)
    j = pl.program_id(3)

    @pl.when(j == 0)
    def _():
        m_scr[...] = jnp.full_like(m_scr, NEG)
        l_scr[...] = jnp.zeros_like(l_scr)
        acc_scr[...] = jnp.zeros_like(acc_scr)

    @pl.when(j <= i)
    def _():
        q = q_ref[...]
        k = k_ref[...]
        v = v_ref[...]
        rr = lax.broadcasted_iota(I32, (tq, tq), 0)
        cc = lax.broadcasted_iota(I32, (tq, tq), 1)
        keep = (rr >= cc) | (j < i)
        for c in range(2):
            s = _dot_nt(q[:, c * DA_DQK:(c + 1) * DA_DQK], k[:, c * DA_DQK:(c + 1) * DA_DQK], prec=None)
            s = jnp.where(keep, s, NEG)
            m_old = m_scr[c]
            m_new = jnp.maximum(m_old, jnp.max(s, axis=-1, keepdims=True))
            p = jnp.exp(s - m_new)
            alpha = jnp.exp(m_old - m_new)
            l_scr[c] = alpha * l_scr[c] + jnp.sum(p, axis=-1, keepdims=True)
            acc_scr[c] = alpha * acc_scr[c] + jnp.dot(p.astype(BF16), v, preferred_element_type=F32)
            m_scr[c] = m_new

    @pl.when(j == i)
    def _():
        o1 = acc_scr[0] / l_scr[0]
        o2 = acc_scr[1] / l_scr[1]
        out_ref[...] = _da_finish(o1, o2, lam_ref[0:1, 0:1], nw_ref[...], post).astype(out_ref.dtype)


def _prompt_attention(q, k, v, lam_t, nw, nb, t, post):
    tq = _row_tile(t, 512)
    nq = t // tq
    kv_spec = pl.BlockSpec((tq, LANES), lambda b, h, i, j: (b * nq + jnp.minimum(i, j), h))
    return pl.pallas_call(
        functools.partial(_attn_kernel, tq, post),
        grid=(nb, DA_HEADS, nq, nq),
        in_specs=[pl.BlockSpec((1, LANES), lambda b, h, i, j: (0, 0)),
                  pl.BlockSpec((tq, LANES), lambda b, h, i, j: (b * nq + i, h)),
                  kv_spec, kv_spec,
                  pl.BlockSpec((1, LANES), lambda b, h, i, j: (0, 0))],
        out_specs=pl.BlockSpec((tq, LANES), lambda b, h, i, j: (b * nq + i, h)),
        out_shape=jax.ShapeDtypeStruct((nb * t, DA_W), BF16),
        scratch_shapes=[pltpu.VMEM((2, tq, 1), F32), pltpu.VMEM((2, tq, 1), F32), pltpu.VMEM((2, tq, LANES), F32)],
        compiler_params=_cparams(("parallel", "parallel", "parallel", "arbitrary"), 32 << 20),
        name="prompt_attention",
    )(lam_t, q, k, v, nw)


def _paged_kernel(s_new, post, pt_ref, lam_ref, qw_ref, kc_ref, vc_ref, kn_ref, vn_ref, nw_ref, out_ref,
                  m_scr, l_scr, acc_scr):
    p_id = pl.program_id(1)
    nrow = DA_HEADS * 2 * s_new
    qw = qw_ref[0]

    @pl.when(p_id == 0)
    def _():
        m_scr[...] = jnp.full_like(m_scr, NEG)
        l_scr[...] = jnp.zeros_like(l_scr)
        acc_scr[...] = jnp.zeros_like(acc_scr)

    def accumulate(kblk, vblk, causal):
        nk = kblk.shape[0]
        s = _dot_nt(qw, kblk.astype(BF16), prec=None)
        row = lax.broadcasted_iota(I32, (nrow, nk), 0)
        col = lax.broadcasted_iota(I32, (nrow, nk), 1)
        keep = (col & (DA_HEADS - 1)) == (row >> ((2 * s_new).bit_length() - 1))
        if causal:
            keep = keep & ((col >> (DA_HEADS.bit_length() - 1)) <= (row & (s_new - 1)))
        s = jnp.where(keep, s, NEG)
        m_old = m_scr[...]
        m_new = jnp.maximum(m_old, jnp.max(s, axis=-1, keepdims=True))
        p = jnp.where(keep, jnp.exp(s - m_new), 0.0)
        alpha = jnp.exp(m_old - m_new)
        l_scr[...] = alpha * l_scr[...] + jnp.sum(p, axis=-1, keepdims=True)
        acc_scr[...] = alpha * acc_scr[...] + jnp.dot(p.astype(BF16), vblk.astype(BF16), preferred_element_type=F32)
        m_scr[...] = m_new

    accumulate(kc_ref[0], vc_ref[0], False)

    @pl.when(p_id == pl.num_programs(1) - 1)
    def _():
        accumulate(kn_ref[0], vn_ref[0], True)
        o = acc_scr[...] / l_scr[...]
        lam = lam_ref[0:1, 0:1]
        for h in range(DA_HEADS):
            o1 = o[h * 2 * s_new:h * 2 * s_new + s_new]
            o2 = o[h * 2 * s_new + s_new:(h + 1) * 2 * s_new]
            out_ref[0, h * s_new:(h + 1) * s_new, :] = _da_finish(o1, o2, lam, nw_ref[...], post)


def _paged_attention(page_table, lam_t, qw, cache_k, cache_v, k_new, v_new, nw, s_new, post):
    nb, n_pages = page_table.shape
    page_rows = cache_k.shape[1]
    nrow = DA_HEADS * 2 * s_new
    new_rows = k_new.shape[1]
    grid_spec = pltpu.PrefetchScalarGridSpec(
        num_scalar_prefetch=1,
        grid=(nb, n_pages),
        in_specs=[pl.BlockSpec((1, LANES), lambda b, p, pt: (0, 0)),
                  pl.BlockSpec((1, nrow, LANES), lambda b, p, pt: (b, 0, 0)),
                  pl.BlockSpec((1, page_rows, LANES), lambda b, p, pt: (pt[b, p], 0, 0)),
                  pl.BlockSpec((1, page_rows, LANES), lambda b, p, pt: (pt[b, p], 0, 0)),
                  pl.BlockSpec((1, new_rows, LANES), lambda b, p, pt: (b, 0, 0)),
                  pl.BlockSpec((1, new_rows, LANES), lambda b, p, pt: (b, 0, 0)),
                  pl.BlockSpec((1, LANES), lambda b, p, pt: (0, 0))],
        out_specs=pl.BlockSpec((1, DA_HEADS * s_new, LANES), lambda b, p, pt: (b, 0, 0)),
        scratch_shapes=[pltpu.VMEM((nrow, 1), F32), pltpu.VMEM((nrow, 1), F32), pltpu.VMEM((nrow, LANES), F32)],
    )
    return pl.pallas_call(
        functools.partial(_paged_kernel, s_new, post),
        grid_spec=grid_spec,
        out_shape=jax.ShapeDtypeStruct((nb, DA_HEADS * s_new, LANES), F32),
        compiler_params=_cparams(("parallel", "arbitrary"), 32 << 20),
        name="paged_attention",
    )(page_table, lam_t, qw, cache_k, cache_v, k_new, v_new, nw)


def _layer_norm(y, g, b):
    mu = jnp.mean(y, axis=-1, keepdims=True)
    d = y - mu
    var = jnp.mean(d * d, axis=-1, keepdims=True)
    return d * lax.rsqrt(var + 1e-5) * g + b


def _mix_kernel(alpha, x_ref, dn_ref, da_ref, w_ref, g_ref, b_ref, x1_ref, x1t_ref):
    mix = (jnp.dot(dn_ref[...], w_ref[0:DN_QK, :], preferred_element_type=F32)
           + jnp.dot(da_ref[...], w_ref[DN_QK:, :], preferred_element_type=F32))
    x1 = _layer_norm(alpha * x_ref[...] + mix, g_ref[...], b_ref[...])
    x1_ref[...] = x1
    for s in range(SUBLANES):
        x1t_ref[:, s, :] = x1[:, s * LANES:(s + 1) * LANES]


def _mix(x2d, dn, da, w_out, g, b, alpha):
    m = x2d.shape[0]
    tm = _row_tile(m, 256)
    row = lambda w: pl.BlockSpec((tm, w), lambda i: (i, 0))
    full = lambda shape: pl.BlockSpec(shape, lambda i: (0,) * len(shape))
    return pl.pallas_call(
        functools.partial(_mix_kernel, alpha),
        grid=(m // tm,),
        in_specs=[row(D_MODEL), row(DN_QK), row(DA_W), full((D_MODEL, D_MODEL)), full((1, D_MODEL)), full((1, D_MODEL))],
        out_specs=[row(D_MODEL), pl.BlockSpec((tm, SUBLANES, LANES), lambda i: (i, 0, 0))],
        out_shape=[jax.ShapeDtypeStruct((m, D_MODEL), F32), jax.ShapeDtypeStruct((m, SUBLANES, LANES), F32)],
        compiler_params=_cparams(("parallel",), 32 << 20),
        name="mix_ln1",
    )(x2d, dn, da, w_out, g, b)


def _top16_rows(vals, payload=None):
    r = vals.shape[0]
    iota = lax.broadcasted_iota(I32, vals.shape, 0)
    tops, idxs = [], []
    for _ in range(PEER_TOPK):
        m = jnp.max(vals, axis=0, keepdims=True)
        am = jnp.min(jnp.where(vals == m, iota, r), axis=0, keepdims=True)
        hit = iota == am
        tops.append(m)
        idxs.append(am if payload is None else jnp.sum(jnp.where(hit, payload, 0), axis=0, keepdims=True))
        vals = jnp.where(hit, -jnp.inf, vals)
    return tops, idxs


def _route_kernel(x_ref, wq_ref, sk_ref, eidx_ref, gate_ref):
    xb = x_ref[...].astype(BF16)
    half = PEER_DK // 2
    for h in range(PEER_HEADS):
        picks = []
        for c in range(2):
            col = (h * 2 + c) * half
            q = jnp.dot(xb, wq_ref[:, col:col + half], preferred_element_type=F32)
            st = _dot_nt(sk_ref[c], q.astype(BF16), prec=None)
            picks.append(_top16_rows(st))
        (s1, i1), (s2, i2) = picks
        s2m = jnp.concatenate(s2, axis=0)
        i2m = jnp.concatenate(i2, axis=0)
        cand = jnp.concatenate([s1[a] + s2m for a in range(PEER_TOPK)], axis=0)
        cidx = jnp.concatenate([i1[a] * N_KEYS + i2m for a in range(PEER_TOPK)], axis=0)
        top, eidx = _top16_rows(cand, cidx)
        ex = [jnp.exp(t - top[0]) for t in top]
        den = ex[0]
        for e in ex[1:]:
            den = den + e
        gate_ref[h * PEER_TOPK:(h + 1) * PEER_TOPK, :] = jnp.concatenate(ex, axis=0) / den
        eidx_ref[h * PEER_TOPK:(h + 1) * PEER_TOPK, :] = jnp.concatenate(eidx, axis=0)


def _route(x1, wq, sk):
    m = x1.shape[0]
    tm = _row_tile(m, 256)
    full = lambda shape: pl.BlockSpec(shape, lambda i: (0,) * len(shape))
    col = pl.BlockSpec((N_PAIRS, tm), lambda i: (0, i))
    return pl.pallas_call(
        _route_kernel,
        grid=(m // tm,),
        in_specs=[pl.BlockSpec((tm, D_MODEL), lambda i: (i, 0)), full(wq.shape), full(sk.shape)],
        out_specs=[col, col],
        out_shape=[jax.ShapeDtypeStruct((N_PAIRS, m), I32), jax.ShapeDtypeStruct((N_PAIRS, m), F32)],
        compiler_params=_cparams(("parallel",), 32 << 20),
        name="peer_route",
    )(x1, wq, sk)


def _expert_row(tab_ref, e):
    word = tab_ref[e & (HALF_EXPERTS - 1)]
    shift = 16 - ((e >> 13) << 4)
    return pltpu.bitcast((word << shift) & jnp.int32(-65536), F32)


def _fold_pair(a, b, s, sub):
    m = (sub & s) == 0
    return jnp.where(m, a, b) + pltpu.roll(jnp.where(m, b, a), s, axis=0)


def _sublane_sums(p, sub):
    c = [_fold_pair(p[2 * i], p[2 * i + 1], 1, sub) for i in range(4)]
    d = [_fold_pair(c[0], c[1], 2, sub), _fold_pair(c[2], c[3], 2, sub)]
    return _fold_pair(d[0], d[1], 4, sub)


def _peer_u_kernel(tb, idx_ref, x_ref, tab_ref, h_ref):
    sub = lax.broadcasted_iota(I32, (SUBLANES, LANES), 0)
    lane = lax.broadcasted_iota(I32, (N_PAIRS, tb), 1)
    ones = jnp.ones((LANES, tb), BF16)

    def body(t, ht):
        xv = x_ref[t]
        groups = []
        for g in range(N_PAIRS // SUBLANES):
            prods = [_expert_row(tab_ref, idx_ref[g * SUBLANES + i, t]) * xv for i in range(SUBLANES)]
            groups.append(_sublane_sums(prods, sub))
        part = jnp.concatenate(groups, axis=0)
        hi = part.astype(BF16)
        lo = (part - hi.astype(F32)).astype(BF16)
        tot = jnp.dot(hi, ones, preferred_element_type=F32) + jnp.dot(lo, ones, preferred_element_type=F32)
        return jnp.where(lane == t, tot, ht)

    h_ref[...] = lax.fori_loop(0, tb, body, jnp.zeros((N_PAIRS, tb), F32))


def _peer_v_kernel(tb, idx_ref, w_ref, tab_ref, out_ref):
    def body(t, carry):
        accs = [jnp.zeros((SUBLANES, LANES), F32) for _ in range(4)]
        for k in range(N_PAIRS):
            accs[k % 4] = accs[k % 4] + _expert_row(tab_ref, idx_ref[k, t]) * w_ref[k, t]
        out_ref[t] = (accs[0] + accs[1]) + (accs[2] + accs[3])
        return carry

    lax.fori_loop(0, tb, body, 0)


def _peer_tb(m):
    return LANES if m % LANES == 0 else m


def _peer_u(eidx, x1t, tab):
    m = x1t.shape[0]
    tb = _peer_tb(m)
    smem_col = pl.BlockSpec((N_PAIRS, tb), lambda i: (0, i), memory_space=pltpu.SMEM)
    return pl.pallas_call(
        functools.partial(_peer_u_kernel, tb),
        grid=(m // tb,),
        in_specs=[smem_col, pl.BlockSpec((tb, SUBLANES, LANES), lambda i: (i, 0, 0)),
                  pl.BlockSpec(tab.shape, lambda i: (0, 0, 0))],
        out_specs=pl.BlockSpec((N_PAIRS, tb), lambda i: (0, i)),
        out_shape=jax.ShapeDtypeStruct((N_PAIRS, m), F32),
        compiler_params=_cparams(("arbitrary",), VMEM_PHYSICAL - (8 << 20)),
        name="peer_u",
    )(eidx, x1t, tab)


def _peer_v(eidx, w, tab):
    m = eidx.shape[1]
    tb = _peer_tb(m)
    smem_col = pl.BlockSpec((N_PAIRS, tb), lambda i: (0, i), memory_space=pltpu.SMEM)
    return pl.pallas_call(
        functools.partial(_peer_v_kernel, tb),
        grid=(m // tb,),
        in_specs=[smem_col, smem_col, pl.BlockSpec(tab.shape, lambda i: (0, 0, 0))],
        out_specs=pl.BlockSpec((tb, SUBLANES, LANES), lambda i: (i, 0, 0)),
        out_shape=jax.ShapeDtypeStruct((m, SUBLANES, LANES), F32),
        compiler_params=_cparams(("arbitrary",), VMEM_PHYSICAL - (8 << 20)),
        name="peer_v",
    )(eidx, w, tab)


def _peer_w_kernel(h_ref, gate_ref, w_ref):
    h = h_ref[...]
    w_ref[...] = gate_ref[...] * (0.5 * h * (1.0 + lax.erf(h * (0.5 ** 0.5))))


def _peer_w(h, gate):
    m = h.shape[1]
    tm = _row_tile(m, 2048) if m % LANES == 0 else m
    col = pl.BlockSpec((N_PAIRS, tm), lambda i: (0, i))
    return pl.pallas_call(
        _peer_w_kernel,
        grid=(m // tm,),
        in_specs=[col, col],
        out_specs=col,
        out_shape=jax.ShapeDtypeStruct((N_PAIRS, m), F32),
        compiler_params=_cparams(("parallel",), 32 << 20),
        name="peer_w",
    )(h, gate)


def _ln2_kernel(alpha, x1_ref, ffn_ref, g_ref, b_ref, y_ref):
    ffn = jnp.concatenate([ffn_ref[:, s, :] for s in range(SUBLANES)], axis=1)
    y_ref[...] = _layer_norm(alpha * x1_ref[...] + ffn, g_ref[...], b_ref[...])


def _ln2(x1, ffn_t, g, b, alpha):
    m = x1.shape[0]
    tm = _row_tile(m, 256)
    row = pl.BlockSpec((tm, D_MODEL), lambda i: (i, 0))
    full = pl.BlockSpec((1, D_MODEL), lambda i: (0, 0))
    return pl.pallas_call(
        functools.partial(_ln2_kernel, alpha),
        grid=(m // tm,),
        in_specs=[row, pl.BlockSpec((tm, SUBLANES, LANES), lambda i: (i, 0, 0)), full, full],
        out_specs=row,
        out_shape=jax.ShapeDtypeStruct((m, D_MODEL), F32),
        compiler_params=_cparams(("parallel",), 32 << 20),
        name="ln2",
    )(x1, ffn_t, g, b)


def _pack_table(tab):
    bits = lax.bitcast_convert_type(tab.astype(BF16), jnp.uint16).astype(jnp.uint32)
    word = bits[:HALF_EXPERTS] | (bits[HALF_EXPERTS:] << 16)
    return lax.bitcast_convert_type(word, I32).reshape(HALF_EXPERTS, SUBLANES, LANES)


def _rope_tables(pos):
    half = DA_DQK // 2
    inv_freq = ROPE_THETA ** (-jnp.arange(half, dtype=F32) * 2.0 / DA_DQK)
    ang = pos.astype(F32)[:, None] * inv_freq[None, :]
    cos, sin = jnp.cos(ang), jnp.sin(ang)
    return jnp.tile(cos, (1, LANES // half)), jnp.tile(jnp.concatenate([-sin, sin], axis=1), (1, LANES // DA_DQK))


def _peer_and_norm(x1, x1t, p):
    eidx, gate = _route(x1, p["wq"], p["sk"])
    h = _peer_u(eidx, x1t, p["u_tab"])
    w = _peer_w(h, gate)
    ffn_t = _peer_v(eidx, w, p["v_tab"])
    return _ln2(x1, ffn_t, p["ln2_g"], p["ln2_b"], p["alpha"])


def kernel(x_prompt, x_sample, cache_k, cache_v, state_dn, state_conv, page_table, w_in, dn_conv_w, dn_a_log,
           dn_dt_bias, dn_norm_w, da_lambda_q1, da_lambda_k1, da_lambda_q2, da_lambda_k2, da_norm_w, w_out,
           ln1_g, ln1_b, peer_wq, peer_sub_keys, peer_u, peer_v, ln2_g, ln2_b):
    depth = w_in.shape[0]
    assert depth == 1, "single-layer trunk"
    nb, t, _ = x_prompt.shape
    nsb, s_new, _ = x_sample.shape
    assert s_new & (s_new - 1) == 0, "sample length must be a power of two"
    n_pages, page = page_table.shape[1], cache_k.shape[2]
    past_len = n_pages * page
    alpha = (2 * depth) ** 0.25
    lam_init = 0.8 - 0.6 * math.exp(-0.3 * 0)
    post = 1.0 - lam_init
    lam = (jnp.exp(jnp.sum(da_lambda_q1[0] * da_lambda_k1[0])) - jnp.exp(jnp.sum(da_lambda_q2[0] * da_lambda_k2[0]))
           + lam_init)
    lam_t = jnp.full((1, LANES), lam, F32)

    o_ab = DN_CH + DN_QK
    w0 = w_in[0]
    w_r = jnp.concatenate([w0[:, :o_ab], w0[:, o_ab + 2 * DN_HEADS:], w0[:, o_ab:o_ab + 2 * DN_HEADS],
                           jnp.zeros((D_MODEL, LANES - 2 * DN_HEADS), F32)], axis=1).astype(BF16)
    hp = jnp.zeros((SUBLANES, LANES), F32).at[0, :DN_HEADS].set(dn_a_log[0]).at[1, :DN_HEADS].set(dn_dt_bias[0])
    dn_nw = dn_norm_w[0].reshape(1, DN_D)
    da_nw = da_norm_w[0].reshape(1, DA_DV)
    p = dict(wq=peer_wq[0].astype(BF16), sk=peer_sub_keys[0].astype(BF16), u_tab=_pack_table(peer_u[0]),
             v_tab=_pack_table(peer_v[0]), ln2_g=ln2_g[0].reshape(1, D_MODEL), ln2_b=ln2_b[0].reshape(1, D_MODEL),
             alpha=alpha)
    w_out_b = w_out[0].astype(BF16)
    g1, b1 = ln1_g[0].reshape(1, D_MODEL), ln1_b[0].reshape(1, D_MODEL)

    xp = x_prompt.reshape(nb * t, D_MODEL)
    cos_p, sin_p = _rope_tables(jnp.tile(jnp.arange(t), nb))
    qkv, z, ab, q, kf, kb, vf, vb = _proj(xp, w_r, cos_p, sin_p)
    c = min(DN_CHUNK, t)
    dn, s_p, c_p = _deltanet(qkv, z, ab, dn_conv_w[0], hp, dn_nw, nb, t, c, c)
    da = _prompt_attention(q, kb, vb, lam_t, da_nw, nb, t, post)
    x1, x1t = _mix(xp, dn, da, w_out_b, g1, b1, alpha)
    y_p = _peer_and_norm(x1, x1t, p).reshape(nb, t, D_MODEL)
    k_p = kf.reshape(1, nb, t, DA_HEADS, 2 * DA_DQK)
    v_p = vf.reshape(1, nb, t, DA_HEADS, DA_DV)

    xs = x_sample.reshape(nsb * s_new, D_MODEL)
    cos_s, sin_s = _rope_tables(jnp.tile(past_len + jnp.arange(s_new), nsb))
    qkv, z, ab, q, kf, kb, vf, vb = _proj(xs, w_r, cos_s, sin_s)
    cs = SUBLANES * (-(-s_new // SUBLANES))
    pad = lambda a: jnp.pad(a.reshape(nsb, s_new, -1), ((0, 0), (0, cs - s_new), (0, 0))).reshape(nsb * cs, -1)
    dn, s_s, c_s = _deltanet(pad(qkv), pad(z), pad(ab), dn_conv_w[0], hp, dn_nw, nsb, cs, cs, s_new,
                             s0=state_dn[0], cb0=state_conv[0])
    dn = dn.reshape(nsb, cs, DN_QK)[:, :s_new].reshape(nsb * s_new, DN_QK)
    q5 = q.reshape(nsb, s_new, DA_HEADS, 2, DA_DQK).transpose(0, 2, 3, 1, 4)
    qw = (q5[:, :, :, :, None, :] * jnp.eye(2, dtype=BF16)[None, None, :, None, :, None]).reshape(
        nsb, DA_HEADS * 2 * s_new, 2 * DA_DQK)
    new_rows = s_new * DA_HEADS
    new_pad = SUBLANES * 2 * (-(-new_rows // (SUBLANES * 2)))
    padn = lambda a: jnp.pad(a.reshape(nsb, new_rows, LANES), ((0, 0), (0, new_pad - new_rows), (0, 0)))
    da = _paged_attention(page_table, lam_t, qw, cache_k[0].reshape(-1, page * DA_HEADS, LANES),
                          cache_v[0].reshape(-1, page * DA_HEADS, LANES), padn(kf), padn(vf), da_nw, s_new, post)
    da = da.reshape(nsb, DA_HEADS, s_new, DA_DV).transpose(0, 2, 1, 3).reshape(nsb * s_new, DA_W).astype(BF16)
    x1, x1t = _mix(xs, dn, da, w_out_b, g1, b1, alpha)
    y_s = _peer_and_norm(x1, x1t, p).reshape(nsb, s_new, D_MODEL)
    k_s = kf.reshape(1, nsb, s_new, DA_HEADS, 2 * DA_DQK)
    v_s = vf.reshape(1, nsb, s_new, DA_HEADS, DA_DV)

    return (y_p, y_s, k_p, v_p, k_s, v_s, s_p[None], c_p[None], s_s[None], c_s[None])
```

```python
import functools
import math

import jax
import jax.numpy as jnp
from jax import lax
from jax.experimental import pallas as pl
from jax.experimental.pallas import tpu as pltpu

F32 = jnp.float32
BF16 = jnp.bfloat16
I32 = jnp.int32
HI = lax.Precision.HIGHEST

LANES = 128
SUBLANES = 8
VMEM_PHYSICAL = 64 * 1024 * 1024

D_MODEL = 1024
DN_HEADS = 4
DN_D = 128
DN_CONV = 4
DN_QK = DN_HEADS * DN_D
DN_CH = 3 * DN_QK
DN_CHUNK = 64
DA_HEADS = 4
DA_DQK = 64
DA_DV = 128
DA_W = DA_HEADS * DA_DV
ROPE_THETA = 10000.0
PEER_HEADS = 8
PEER_DK = 256
N_KEYS = 128
PEER_TOPK = 16
N_PAIRS = PEER_HEADS * PEER_TOPK
HALF_EXPERTS = N_KEYS * N_KEYS // 2
PROJ_W = DN_CH + DN_QK + 3 * DA_W + LANES
NEG = -1e30
PAGES_PER_STEP = 8
U_ROW_SHIFT = 5
ROW_MASK = 0xFFF8
HI_HALF = -65536
PAIR_GROUP = 32


def _cparams(sem, vmem_bytes):
    return pltpu.CompilerParams(dimension_semantics=sem, vmem_limit_bytes=int(vmem_bytes))


def _row_tile(m, cap):
    t = cap
    while m % t:
        t //= 2
    assert t >= SUBLANES, (m, cap)
    return t


def _rope_cols(t, cos, sin_signed):
    lane = lax.broadcasted_iota(I32, (t.shape[0], LANES), 1)
    first = (lane & (DA_DQK - 1)) < (DA_DQK // 2)
    outs = []
    for h in range(DA_HEADS):
        th = t[:, h * LANES:(h + 1) * LANES]
        partner = jnp.where(first, pltpu.roll(th, LANES - DA_DQK // 2, axis=1), pltpu.roll(th, DA_DQK // 2, axis=1))
        outs.append(th * cos + partner * sin_signed)
    return jnp.concatenate(outs, axis=1)


def _proj_kernel(x_ref, w_ref, cos_ref, sin_ref, qkv_ref, z_ref, ab_ref, q_ref, kf_ref, kb_ref, vf_ref, vb_ref):
    xb = x_ref[...].astype(BF16)

    def mm(lo, hi):
        return jnp.dot(xb, w_ref[:, lo:hi], preferred_element_type=F32)

    qkv_ref[...] = mm(0, DN_CH)
    o = DN_CH
    z_ref[...] = mm(o, o + DN_QK)
    o += DN_QK
    cos = cos_ref[...]
    sin = sin_ref[...]
    q = _rope_cols(mm(o, o + DA_W), cos, sin)
    q_ref[...] = (q * (DA_DQK ** -0.5)).astype(BF16)
    o += DA_W
    k = _rope_cols(mm(o, o + DA_W), cos, sin)
    kf_ref[...] = k
    kb_ref[...] = k.astype(BF16)
    o += DA_W
    v = mm(o, o + DA_W)
    vf_ref[...] = v
    vb_ref[...] = v.astype(BF16)
    o += DA_W
    ab_ref[...] = mm(o, o + LANES)


def _proj(x2d, w_r, cos_t, sin_t):
    m = x2d.shape[0]
    tm = _row_tile(m, 256)
    row = lambda w: pl.BlockSpec((tm, w), lambda i: (i, 0))
    outs = [(DN_CH, F32), (DN_QK, F32), (LANES, F32), (DA_W, BF16), (DA_W, F32), (DA_W, BF16), (DA_W, F32), (DA_W, BF16)]
    return pl.pallas_call(
        _proj_kernel,
        grid=(m // tm,),
        in_specs=[row(D_MODEL), pl.BlockSpec((D_MODEL, PROJ_W), lambda i: (0, 0)), row(LANES), row(LANES)],
        out_specs=[row(w) for w, _ in outs],
        out_shape=[jax.ShapeDtypeStruct((m, w), d) for w, d in outs],
        compiler_params=_cparams(("parallel",), 48 << 20),
        name="proj",
    )(x2d, w_r, cos_t, sin_t)


def _dot(a, b, prec=HI):
    return jnp.dot(a, b, precision=prec, preferred_element_type=F32)


def _dot_nt(a, b, prec=HI):
    return lax.dot_general(a, b, (((1,), (1,)), ((), ())), precision=prec, preferred_element_type=F32)


def _dot1(a, b):
    return jnp.dot(a.astype(BF16), b.astype(BF16), preferred_element_type=F32)


def _dot1_nt(a, b):
    return lax.dot_general(a.astype(BF16), b.astype(BF16), (((1,), (1,)), ((), ())), preferred_element_type=F32)


def _dot1_tn(a, b):
    return lax.dot_general(a.astype(BF16), b.astype(BF16), (((0,), (0,)), ((), ())), preferred_element_type=F32)


def _split_bf16(a):
    hi = a.astype(BF16)
    return hi, (a - hi.astype(F32)).astype(BF16)


def _dot3(a, b):
    ah, al = _split_bf16(a)
    bh, bl = _split_bf16(b)
    return (jnp.dot(ah, bh, preferred_element_type=F32)
            + (jnp.dot(ah, bl, preferred_element_type=F32) + jnp.dot(al, bh, preferred_element_type=F32)))


def _unit_lower_inverse(a, c):
    r = lax.broadcasted_iota(I32, (c, c), 0)
    s = lax.broadcasted_iota(I32, (c, c), 1)
    base = min(c, 16)
    lb = base.bit_length() - 1
    n = jnp.where((r >> lb) == (s >> lb), -a, 0.0)
    p = jnp.where(r == s, 1.0, 0.0) + n
    npow = n
    for _ in range(lb - 1):
        npow = _dot3(npow, npow)
        p = p + _dot3(p, npow)
    while base < c:
        lb = base.bit_length() - 1
        e_mask = ((r >> (lb + 1)) == (s >> (lb + 1))) & (((r >> lb) & 1) == 1) & (((s >> lb) & 1) == 0)
        e = jnp.where(e_mask, a, 0.0)
        p = p - _dot3(_dot3(p, e), p)
        base *= 2
    return p


def _silu(x):
    return x * (1.0 / (1.0 + jnp.exp(-x)))


def _dn_kernel(c, nvalid, has_state, *refs):
    if has_state:
        (qkv_ref, z_ref, ab_ref, cw_ref, hp_ref, nw_ref, s0_ref, cb0_ref,
         out_ref, sfin_ref, cfin_ref, s_scr, xbuf) = refs
    else:
        (qkv_ref, z_ref, ab_ref, cw_ref, hp_ref, nw_ref,
         out_ref, sfin_ref, cfin_ref, s_scr, xbuf) = refs
    i = pl.program_id(1)
    last = pl.num_programs(1) - 1
    base = SUBLANES
    hist = DN_CONV - 1

    @pl.when(i == 0)
    def _():
        if has_state:
            s_scr[...] = s0_ref[0]
            xbuf[base - hist:base, :] = cb0_ref[0]
        else:
            s_scr[...] = jnp.zeros_like(s_scr)
            xbuf[0:base, :] = jnp.zeros((base, DN_CH), F32)

    xbuf[base:base + c, :] = qkv_ref[...]
    acc = xbuf[base - hist:base - hist + c, :] * cw_ref[0:1, :]
    for w in range(1, DN_CONV):
        acc = acc + xbuf[base - hist + w:base - hist + w + c, :] * cw_ref[w:w + 1, :]
    act = _silu(acc)

    @pl.when(i == last)
    def _():
        cfin_ref[0] = xbuf[base + nvalid - hist:base + nvalid, :]

    xbuf[base - hist:base, :] = xbuf[base + c - hist:base + c, :]

    ab = ab_ref[...]
    rowi = lax.broadcasted_iota(I32, (c, LANES), 0)
    live = rowi < nvalid
    sp_in = ab + hp_ref[1:2, :]
    softplus = jnp.maximum(sp_in, 0.0) + jnp.log(1.0 + jnp.exp(-jnp.abs(sp_in)))
    g_all = jnp.where(live, -jnp.exp(hp_ref[0:1, :]) * softplus, 0.0)
    beta_all = jnp.where(live, 1.0 / (1.0 + jnp.exp(-ab)), 0.0)
    rr = lax.broadcasted_iota(I32, (c, c), 0)
    cc = lax.broadcasted_iota(I32, (c, c), 1)
    incl = rr >= cc
    strict = rr > cc
    gcum = _dot(jnp.where(incl, 1.0, 0.0), g_all)
    sel = jnp.where(lax.broadcasted_iota(I32, (SUBLANES, LANES), 0) == lax.broadcasted_iota(I32, (SUBLANES, LANES), 1),
                    1.0, 0.0)
    grows = _dot_nt(sel, gcum)
    heads = range(DN_HEADS)

    qs, ks, vs = [], [], []
    for h in heads:
        qh = act[:, h * DN_D:(h + 1) * DN_D]
        kh = act[:, DN_QK + h * DN_D:DN_QK + (h + 1) * DN_D]
        qs.append(qh * lax.rsqrt(jnp.sum(qh * qh, axis=-1, keepdims=True) + 1e-6) * (DN_D ** -0.5))
        ks.append(kh * lax.rsqrt(jnp.sum(kh * kh, axis=-1, keepdims=True) + 1e-6))
        vs.append(act[:, 2 * DN_QK + h * DN_D:2 * DN_QK + (h + 1) * DN_D])
    betas = [beta_all[:, DN_HEADS + h:DN_HEADS + h + 1] for h in heads]
    gcols = [gcum[:, h:h + 1] for h in heads]
    glasts = [gcum[c - 1:c, h:h + 1] for h in heads]
    decays = [jnp.exp(jnp.where(incl, gcols[h] - grows[h:h + 1, :], NEG)) for h in heads]
    egs = [jnp.exp(g) for g in gcols]
    kbs = [ks[h] * betas[h] for h in heads]
    amats = [jnp.where(strict, _dot1_nt(kbs[h], ks[h]) * decays[h], 0.0) for h in heads]
    tinvs = [_unit_lower_inverse(a, c) for a in amats]
    us = [_dot3(tinvs[h], vs[h] * betas[h]) for h in heads]
    ws = [_dot3(tinvs[h], kbs[h] * egs[h]) for h in heads]
    aqks = [_dot1_nt(qs[h], ks[h]) * decays[h] for h in heads]
    ss = [s_scr[h] for h in heads]
    v_news = [us[h] - _dot1(ws[h], ss[h]) for h in heads]
    outs = [_dot1(qs[h] * egs[h], ss[h]) + _dot1(aqks[h], v_news[h]) for h in heads]
    for h in heads:
        s_scr[h] = ss[h] * jnp.exp(glasts[h]) + _dot1_tn(ks[h] * jnp.exp(glasts[h] - gcols[h]), v_news[h])
    for h in heads:
        o = outs[h]
        zh = z_ref[:, h * DN_D:(h + 1) * DN_D]
        on = o * lax.rsqrt(jnp.mean(o * o, axis=-1, keepdims=True) + 1e-6) * nw_ref[...]
        out_ref[:, h * DN_D:(h + 1) * DN_D] = (on * _silu(zh)).astype(out_ref.dtype)

    @pl.when(i == last)
    def _():
        sfin_ref[0] = s_scr[...]


def _deltanet(qkv, z, ab, conv_w, hp, norm_w, nb, t_rows, c, nvalid, s0=None, cb0=None):
    nc = t_rows // c
    has_state = s0 is not None
    row = lambda w: pl.BlockSpec((c, w), lambda b, i: (b * nc + i, 0))
    full = lambda shape: pl.BlockSpec(shape, lambda b, i: (0,) * len(shape))
    in_specs = [row(DN_CH), row(DN_QK), row(LANES), full((DN_CONV, DN_CH)), full((SUBLANES, LANES)), full((1, DN_D))]
    args = [qkv, z, ab, conv_w, hp, norm_w]
    if has_state:
        in_specs += [pl.BlockSpec((1, DN_HEADS, DN_D, DN_D), lambda b, i: (b, 0, 0, 0)),
                     pl.BlockSpec((1, DN_CONV - 1, DN_CH), lambda b, i: (b, 0, 0))]
        args += [s0, cb0]
    return pl.pallas_call(
        functools.partial(_dn_kernel, c, nvalid, has_state),
        grid=(nb, nc),
        in_specs=in_specs,
        out_specs=[row(DN_QK),
                   pl.BlockSpec((1, DN_HEADS, DN_D, DN_D), lambda b, i: (b, 0, 0, 0)),
                   pl.BlockSpec((1, DN_CONV - 1, DN_CH), lambda b, i: (b, 0, 0))],
        out_shape=[jax.ShapeDtypeStruct((nb * t_rows, DN_QK), BF16),
                   jax.ShapeDtypeStruct((nb, DN_HEADS, DN_D, DN_D), F32),
                   jax.ShapeDtypeStruct((nb, DN_CONV - 1, DN_CH), F32)],
        scratch_shapes=[pltpu.VMEM((DN_HEADS, DN_D, DN_D), F32), pltpu.VMEM((c + SUBLANES, DN_CH), F32)],
        compiler_params=_cparams(("parallel", "arbitrary"), 32 << 20),
        name="deltanet",
    )(*args)


def _da_finish(o1, o2, lam, nw, post):
    o = o1 - lam * o2
    return o * lax.rsqrt(jnp.mean(o * o, axis=-1, keepdims=True) + 1e-6) * nw * post


def _attn_kernel(tq, post, lam_ref, q_ref, k_ref, v_ref, nw_ref, out_ref, m_scr, acc_scr):
    i = pl.program_id(2)
    j = pl.program_id(3)
    nt = tq // LANES

    @pl.when(j == 0)
    def _():
        m_scr[...] = jnp.full_like(m_scr, NEG)
        acc_scr[...] = jnp.zeros_like(acc_scr)

    def update(masked):
        q = q_ref[...]
        k = k_ref[...]
        v1 = jnp.concatenate([v_ref[...], jnp.ones((tq, LANES), BF16)], axis=1)
        if masked:
            keep = lax.broadcasted_iota(I32, (tq, tq), 0) >= lax.broadcasted_iota(I32, (tq, tq), 1)
        for c in range(2):
            s = _dot_nt(q[:, c * DA_DQK:(c + 1) * DA_DQK], k[:, c * DA_DQK:(c + 1) * DA_DQK], prec=None)
            if masked:
                s = jnp.where(keep, s, NEG)
            m_old = m_scr[c]
            m_new = jnp.maximum(m_old, jnp.max(s, axis=-1, keepdims=True))
            p = jnp.concatenate([jnp.exp(s[:, a * LANES:(a + 1) * LANES] - m_new) for a in range(nt)], axis=1)
            alpha = jnp.exp(m_old - m_new)
            acc_scr[c] = (jnp.concatenate([alpha, alpha], axis=1) * acc_scr[c]
                          + jnp.dot(p.astype(BF16), v1, preferred_element_type=F32))
            m_scr[c] = m_new

    @pl.when(j < i)
    def _():
        update(False)

    @pl.when(j == i)
    def _():
        update(True)
        o1 = acc_scr[0, :, 0:LANES] / acc_scr[0, :, LANES:]
        o2 = acc_scr[1, :, 0:LANES] / acc_scr[1, :, LANES:]
        out_ref[...] = _da_finish(o1, o2, lam_ref[0:1, 0:1], nw_ref[...], post).astype(out_ref.dtype)


def _prompt_attention(q, k, v, lam_t, nw, nb, t, post):
    tq = _row_tile(t, 512)
    nq = t // tq
    kv_spec = pl.BlockSpec((tq, LANES), lambda b, h, i, j: (b * nq + jnp.minimum(i, j), h))
    return pl.pallas_call(
        functools.partial(_attn_kernel, tq, post),
        grid=(nb, DA_HEADS, nq, nq),
        in_specs=[pl.BlockSpec((1, LANES), lambda b, h, i, j: (0, 0)),
                  pl.BlockSpec((tq, LANES), lambda b, h, i, j: (b * nq + i, h)),
                  kv_spec, kv_spec,
                  pl.BlockSpec((1, LANES), lambda b, h, i, j: (0, 0))],
        out_specs=pl.BlockSpec((tq, LANES), lambda b, h, i, j: (b * nq + i, h)),
        out_shape=jax.ShapeDtypeStruct((nb * t, DA_W), BF16),
        scratch_shapes=[pltpu.VMEM((2, tq, LANES), F32), pltpu.VMEM((2, tq, 2 * LANES), F32)],
        compiler_params=_cparams(("parallel", "parallel", "parallel", "arbitrary"), 32 << 20),
        name="prompt_attention",
    )(lam_t, q, k, v, nw)


def _paged_kernel(s_new, post, npg, pt_ref, lam_ref, qw_ref, *refs):
    k_refs, v_refs = refs[:npg], refs[npg:2 * npg]
    kn_ref, vn_ref, nw_ref, out_ref, m_scr, l_scr, acc_scr = refs[2 * npg:]
    p_id = pl.program_id(1)
    nrow = DA_HEADS * 2 * s_new
    qw = qw_ref[0]

    @pl.when(p_id == 0)
    def _():
        m_scr[...] = jnp.full_like(m_scr, NEG)
        l_scr[...] = jnp.zeros_like(l_scr)
        acc_scr[...] = jnp.zeros_like(acc_scr)

    def keep_mask(nk, causal):
        row = lax.broadcasted_iota(I32, (nrow, nk), 0)
        col = lax.broadcasted_iota(I32, (nrow, nk), 1)
        keep = (col & (DA_HEADS - 1)) == (row >> ((2 * s_new).bit_length() - 1))
        if causal:
            keep = keep & ((col >> (DA_HEADS.bit_length() - 1)) <= (row & (s_new - 1)))
        return keep

    def accumulate(kblks, vblks, keep):
        ss = [jnp.where(keep, _dot_nt(qw, kb.astype(BF16), prec=None), NEG) for kb in kblks]
        smax = ss[0]
        for s in ss[1:]:
            smax = jnp.maximum(smax, s)
        m_old = m_scr[...]
        m_new = jnp.maximum(m_old, jnp.max(smax, axis=-1, keepdims=True))
        ps = [jnp.where(keep, jnp.exp(s - m_new), 0.0) for s in ss]
        psum = ps[0]
        for p in ps[1:]:
            psum = psum + p
        alpha = jnp.exp(m_old - m_new)
        l_scr[...] = alpha * l_scr[...] + jnp.sum(psum, axis=-1, keepdims=True)
        pv = jnp.dot(ps[0].astype(BF16), vblks[0].astype(BF16), preferred_element_type=F32)
        for p, vb in zip(ps[1:], vblks[1:]):
            pv = pv + jnp.dot(p.astype(BF16), vb.astype(BF16), preferred_element_type=F32)
        acc_scr[...] = alpha * acc_scr[...] + pv
        m_scr[...] = m_new

    accumulate([r[0] for r in k_refs], [r[0] for r in v_refs], keep_mask(k_refs[0].shape[1], False))

    @pl.when(p_id == pl.num_programs(1) - 1)
    def _():
        accumulate([kn_ref[0]], [vn_ref[0]], keep_mask(kn_ref.shape[1], True))
        o = acc_scr[...] / l_scr[...]
        lam = lam_ref[0:1, 0:1]
        for h in range(DA_HEADS):
            o1 = o[h * 2 * s_new:h * 2 * s_new + s_new]
            o2 = o[h * 2 * s_new + s_new:(h + 1) * 2 * s_new]
            out_ref[0, h * s_new:(h + 1) * s_new, :] = _da_finish(o1, o2, lam, nw_ref[...], post)


def _paged_attention(page_table, lam_t, qw, cache_k, cache_v, k_new, v_new, nw, s_new, post):
    nb, n_pages = page_table.shape
    page_rows = cache_k.shape[1]
    nrow = DA_HEADS * 2 * s_new
    new_rows = k_new.shape[1]
    npg = math.gcd(n_pages, PAGES_PER_STEP)

    def page_spec(i):
        return pl.BlockSpec((1, page_rows, LANES), lambda b, p, pt: (pt[b, p * npg + i], 0, 0))

    grid_spec = pltpu.PrefetchScalarGridSpec(
        num_scalar_prefetch=1,
        grid=(nb, n_pages // npg),
        in_specs=[pl.BlockSpec((1, LANES), lambda b, p, pt: (0, 0)),
                  pl.BlockSpec((1, nrow, LANES), lambda b, p, pt: (b, 0, 0))]
                 + [page_spec(i) for i in range(npg)] * 2
                 + [pl.BlockSpec((1, new_rows, LANES), lambda b, p, pt: (b, 0, 0)),
                    pl.BlockSpec((1, new_rows, LANES), lambda b, p, pt: (b, 0, 0)),
                    pl.BlockSpec((1, LANES), lambda b, p, pt: (0, 0))],
        out_specs=pl.BlockSpec((1, DA_HEADS * s_new, LANES), lambda b, p, pt: (b, 0, 0)),
        scratch_shapes=[pltpu.VMEM((nrow, 1), F32), pltpu.VMEM((nrow, 1), F32), pltpu.VMEM((nrow, LANES), F32)],
    )
    return pl.pallas_call(
        functools.partial(_paged_kernel, s_new, post, npg),
        grid_spec=grid_spec,
        out_shape=jax.ShapeDtypeStruct((nb, DA_HEADS * s_new, LANES), F32),
        compiler_params=_cparams(("parallel", "arbitrary"), 32 << 20),
        name="paged_attention",
    )(page_table, lam_t, qw, *([cache_k] * npg), *([cache_v] * npg), k_new, v_new, nw)


def _layer_norm(y, g, b):
    mu = jnp.mean(y, axis=-1, keepdims=True)
    d = y - mu
    var = jnp.mean(d * d, axis=-1, keepdims=True)
    return d * lax.rsqrt(var + 1e-5) * g + b


def _mix_kernel(alpha, x_ref, dn_ref, da_ref, w_ref, g_ref, b_ref, x1_ref, x1t_ref):
    mix = (jnp.dot(dn_ref[...], w_ref[0:DN_QK, :], preferred_element_type=F32)
           + jnp.dot(da_ref[...], w_ref[DN_QK:, :], preferred_element_type=F32))
    x1 = _layer_norm(alpha * x_ref[...] + mix, g_ref[...], b_ref[...])
    x1_ref[...] = x1
    for s in range(SUBLANES):
        x1t_ref[:, s, :] = x1[:, s * LANES:(s + 1) * LANES]


def _mix(x2d, dn, da, w_out, g, b, alpha):
    m = x2d.shape[0]
    tm = _row_tile(m, 256)
    row = lambda w: pl.BlockSpec((tm, w), lambda i: (i, 0))
    full = lambda shape: pl.BlockSpec(shape, lambda i: (0,) * len(shape))
    return pl.pallas_call(
        functools.partial(_mix_kernel, alpha),
        grid=(m // tm,),
        in_specs=[row(D_MODEL), row(DN_QK), row(DA_W), full((D_MODEL, D_MODEL)), full((1, D_MODEL)), full((1, D_MODEL))],
        out_specs=[row(D_MODEL), pl.BlockSpec((tm, SUBLANES, LANES), lambda i: (i, 0, 0))],
        out_shape=[jax.ShapeDtypeStruct((m, D_MODEL), F32), jax.ShapeDtypeStruct((m, SUBLANES, LANES), F32)],
        compiler_params=_cparams(("parallel",), 32 << 20),
        name="mix_ln1",
    )(x2d, dn, da, w_out, g, b)


def _top16_rows(vals, payload=None):
    r = vals.shape[0]
    iota = lax.broadcasted_iota(I32, vals.shape, 0)
    tops, idxs = [], []
    for _ in range(PEER_TOPK):
        m = jnp.max(vals, axis=0, keepdims=True)
        am = jnp.min(jnp.where(vals == m, iota, r), axis=0, keepdims=True)
        hit = iota == am
        tops.append(m)
        idxs.append(am if payload is None else jnp.sum(jnp.where(hit, payload, 0), axis=0, keepdims=True))
        vals = jnp.where(hit, -jnp.inf, vals)
    return tops, idxs


def _route_kernel(x_ref, wq_ref, sk_ref, code_ref, gate_ref):
    xb = x_ref[...].astype(BF16)
    half = PEER_DK // 2
    gates, codes = [], []
    for h in range(PEER_HEADS):
        picks = []
        for c in range(2):
            col = (h * 2 + c) * half
            q = jnp.dot(xb, wq_ref[:, col:col + half], preferred_element_type=F32)
            st = _dot_nt(sk_ref[c], q.astype(BF16), prec=None)
            picks.append(_top16_rows(st))
        (s1, i1), (s2, i2) = picks
        s2m = jnp.concatenate(s2, axis=0)
        i2m = jnp.concatenate(i2, axis=0)
        sub8 = lax.broadcasted_iota(I32, (SUBLANES, s2m.shape[1]), 0)
        cand = [s1[0] + s2m]
        cidx = [i1[0] * N_KEYS + i2m]
        for a in range(1, SUBLANES):
            cand.append(jnp.where(sub8 < PEER_TOPK // (a + 1), s1[a] + s2m[:SUBLANES], -jnp.inf))
            cidx.append(i1[a] * N_KEYS + i2m[:SUBLANES])
        cand.append(jnp.concatenate(s1[SUBLANES:], axis=0) + s2[0])
        cidx.append(jnp.concatenate(i1[SUBLANES:], axis=0) * N_KEYS + i2[0])
        top, eidx = _top16_rows(jnp.concatenate(cand, axis=0), jnp.concatenate(cidx, axis=0))
        ex = [jnp.exp(t - top[0]) for t in top]
        den = ex[0]
        for e in ex[1:]:
            den = den + e
        gates.append(jnp.concatenate(ex, axis=0) / den)
        e = jnp.concatenate(eidx, axis=0)
        codes.append(((e & (HALF_EXPERTS - 1)) << (U_ROW_SHIFT + 3)) | (16 - ((e >> 13) << 4)))
    gate_ref[...] = jnp.concatenate(gates, axis=0).T
    code_ref[...] = jnp.concatenate(codes, axis=0).T


def _route(x1, wq, sk):
    m = x1.shape[0]
    tm = _row_tile(m, 256)
    full = lambda shape: pl.BlockSpec(shape, lambda i: (0,) * len(shape))
    row = pl.BlockSpec((tm, N_PAIRS), lambda i: (i, 0))
    return pl.pallas_call(
        _route_kernel,
        grid=(m // tm,),
        in_specs=[pl.BlockSpec((tm, D_MODEL), lambda i: (i, 0)), full(wq.shape), full(sk.shape)],
        out_specs=[row, row],
        out_shape=[jax.ShapeDtypeStruct((m, N_PAIRS), I32), jax.ShapeDtypeStruct((m, N_PAIRS), F32)],
        compiler_params=_cparams(("parallel",), 32 << 20),
        name="peer_route",
    )(x1, wq, sk)


def _expert_row(tab_ref, row, shift_v):
    word = tab_ref[pl.ds(pl.multiple_of(row, SUBLANES), SUBLANES), :]
    return pltpu.bitcast((word << shift_v) & HI_HALF, F32)


def _splat(s):
    return jnp.full((SUBLANES, LANES), s, I32)


def _fold_pair(a, b, s, sub):
    m = (sub & s) == 0
    return jnp.where(m, a, b) + pltpu.roll(jnp.where(m, b, a), s, axis=0)


def _sublane_sums(p, sub):
    c = [_fold_pair(p[2 * i], p[2 * i + 1], 1, sub) for i in range(4)]
    d = [_fold_pair(c[0], c[1], 2, sub), _fold_pair(c[2], c[3], 2, sub)]
    return _fold_pair(d[0], d[1], 4, sub)


def _peer_u_kernel(tb, code_ref, x_ref, gate_ref, codev_ref, tab_ref, wc_ref, part_scr, h_scr):
    sub = lax.broadcasted_iota(I32, (SUBLANES, LANES), 0)
    ones = jnp.ones((SUBLANES, LANES), BF16)
    ngrp = N_PAIRS // PAIR_GROUP
    oct_tokens = SUBLANES

    def octet(o, carry):
        def step(it, carry):
            tt = lax.shift_right_logical(it, ngrp.bit_length() - 1)
            g = it & (ngrp - 1)
            t = o * oct_tokens + tt
            xv = x_ref[t]
            for j in range(PAIR_GROUP // SUBLANES):
                k0 = g * PAIR_GROUP + j * SUBLANES
                codes = [code_ref[t * N_PAIRS + k0 + i] for i in range(SUBLANES)]
                prods = [_expert_row(tab_ref, lax.shift_right_logical(c, U_ROW_SHIFT), _splat(c) & 31) * xv
                         for c in codes]
                part_scr[pl.ds(pl.multiple_of(tt * N_PAIRS + k0, SUBLANES), SUBLANES), :] = _sublane_sums(prods, sub)
            return carry

        lax.fori_loop(0, oct_tokens * ngrp, step, 0)
        hi, lo = _split_bf16(part_scr[...])
        tot = (lax.dot_general(ones, hi, (((1,), (1,)), ((), ())), preferred_element_type=F32)
               + lax.dot_general(ones, lo, (((1,), (1,)), ((), ())), preferred_element_type=F32))
        tile = tot[:, 0:N_PAIRS]
        for j in range(1, oct_tokens):
            tile = jnp.where(sub == j, tot[:, j * N_PAIRS:(j + 1) * N_PAIRS], tile)
        h_scr[pl.ds(pl.multiple_of(o * oct_tokens, SUBLANES), SUBLANES), :] = tile
        return carry

    lax.fori_loop(0, tb // oct_tokens, octet, 0)
    h = h_scr[...]
    w = gate_ref[...] * (0.5 * h * (1.0 + lax.erf(h * (0.5 ** 0.5))))
    cv = codev_ref[...]
    code_v = lax.shift_right_logical(cv, U_ROW_SHIFT) | ((cv >> 4) & 1)
    wc_ref[...] = pltpu.bitcast(w.astype(BF16).astype(F32), I32) | code_v


def _peer_v_kernel(tb, wc_ref, tab_ref, out_ref):
    nacc = 4

    def token(t, carry):
        def group(g, accs):
            accs = list(accs)
            for i in range(PAIR_GROUP):
                wc = wc_ref[t * N_PAIRS + g * PAIR_GROUP + i]
                wc_v = _splat(wc)
                weight = pltpu.bitcast(wc_v & HI_HALF, F32)
                row = _expert_row(tab_ref, wc & ROW_MASK, (wc_v & 1) << 4)
                accs[i % nacc] = accs[i % nacc] + row * weight
            return tuple(accs)

        accs = lax.fori_loop(0, N_PAIRS // PAIR_GROUP, group,
                             tuple(jnp.zeros((SUBLANES, LANES), F32) for _ in range(nacc)))
        out_ref[t] = (accs[0] + accs[1]) + (accs[2] + accs[3])
        return carry

    lax.fori_loop(0, tb, token, 0)


def _peer_tb(m):
    return LANES if m % LANES == 0 else m


def _peer_u(code, x1t, gate, tab):
    m = x1t.shape[0]
    tb = _peer_tb(m)
    row = pl.BlockSpec((tb, N_PAIRS), lambda i: (i, 0))
    return pl.pallas_call(
        functools.partial(_peer_u_kernel, tb),
        grid=(m // tb,),
        in_specs=[pl.BlockSpec((tb * N_PAIRS,), lambda i: (i,), memory_space=pltpu.SMEM),
                  pl.BlockSpec((tb, SUBLANES, LANES), lambda i: (i, 0, 0)), row, row,
                  pl.BlockSpec(tab.shape, lambda i: (0, 0))],
        out_specs=row,
        out_shape=jax.ShapeDtypeStruct((m, N_PAIRS), I32),
        scratch_shapes=[pltpu.VMEM((SUBLANES * N_PAIRS, LANES), F32), pltpu.VMEM((tb, N_PAIRS), F32)],
        compiler_params=_cparams(("arbitrary",), VMEM_PHYSICAL - (8 << 20)),
        name="peer_u",
    )(code.reshape(-1), x1t, gate, code, tab)


def _peer_v(wc, tab):
    m = wc.shape[0]
    tb = _peer_tb(m)
    return pl.pallas_call(
        functools.partial(_peer_v_kernel, tb),
        grid=(m // tb,),
        in_specs=[pl.BlockSpec((tb * N_PAIRS,), lambda i: (i,), memory_space=pltpu.SMEM),
                  pl.BlockSpec(tab.shape, lambda i: (0, 0))],
        out_specs=pl.BlockSpec((tb, SUBLANES, LANES), lambda i: (i, 0, 0)),
        out_shape=jax.ShapeDtypeStruct((m, SUBLANES, LANES), F32),
        compiler_params=_cparams(("arbitrary",), VMEM_PHYSICAL - (8 << 20)),
        name="peer_v",
    )(wc.reshape(-1), tab)


def _ln2_kernel(alpha, x1_ref, ffn_ref, g_ref, b_ref, y_ref):
    ffn = jnp.concatenate([ffn_ref[:, s, :] for s in range(SUBLANES)], axis=1)
    y_ref[...] = _layer_norm(alpha * x1_ref[...] + ffn, g_ref[...], b_ref[...])


def _ln2(x1, ffn_t, g, b, alpha):
    m = x1.shape[0]
    tm = _row_tile(m, 256)
    row = pl.BlockSpec((tm, D_MODEL), lambda i: (i, 0))
    full = pl.BlockSpec((1, D_MODEL), lambda i: (0, 0))
    return pl.pallas_call(
        functools.partial(_ln2_kernel, alpha),
        grid=(m // tm,),
        in_specs=[row, pl.BlockSpec((tm, SUBLANES, LANES), lambda i: (i, 0, 0)), full, full],
        out_specs=row,
        out_shape=jax.ShapeDtypeStruct((m, D_MODEL), F32),
        compiler_params=_cparams(("parallel",), 32 << 20),
        name="ln2",
    )(x1, ffn_t, g, b)


def _pack_table(tab):
    bits = lax.bitcast_convert_type(tab.astype(BF16), jnp.uint16).astype(jnp.uint32)
    word = bits[:HALF_EXPERTS] | (bits[HALF_EXPERTS:] << 16)
    return lax.bitcast_convert_type(word, I32).reshape(HALF_EXPERTS * SUBLANES, LANES)


def _rope_tables(pos):
    half = DA_DQK // 2
    inv_freq = ROPE_THETA ** (-jnp.arange(half, dtype=F32) * 2.0 / DA_DQK)
    ang = pos.astype(F32)[:, None] * inv_freq[None, :]
    cos, sin = jnp.cos(ang), jnp.sin(ang)
    return jnp.tile(cos, (1, LANES // half)), jnp.tile(jnp.concatenate([-sin, sin], axis=1), (1, LANES // DA_DQK))


def _peer_and_norm(x1, x1t, p):
    code, gate = _route(x1, p["wq"], p["sk"])
    wc = _peer_u(code, x1t, gate, p["u_tab"])
    ffn_t = _peer_v(wc, p["v_tab"])
    return _ln2(x1, ffn_t, p["ln2_g"], p["ln2_b"], p["alpha"])


def kernel(x_prompt, x_sample, cache_k, cache_v, state_dn, state_conv, page_table, w_in, dn_conv_w, dn_a_log,
           dn_dt_bias, dn_norm_w, da_lambda_q1, da_lambda_k1, da_lambda_q2, da_lambda_k2, da_norm_w, w_out,
           ln1_g, ln1_b, peer_wq, peer_sub_keys, peer_u, peer_v, ln2_g, ln2_b):
    depth = w_in.shape[0]
    assert depth == 1, "single-layer trunk"
    nb, t, _ = x_prompt.shape
    nsb, s_new, _ = x_sample.shape
    assert s_new & (s_new - 1) == 0, "sample length must be a power of two"
    n_pages, page = page_table.shape[1], cache_k.shape[2]
    past_len = n_pages * page
    alpha = (2 * depth) ** 0.25
    lam_init = 0.8 - 0.6 * math.exp(-0.3 * 0)
    post = 1.0 - lam_init
    lam = (jnp.exp(jnp.sum(da_lambda_q1[0] * da_lambda_k1[0])) - jnp.exp(jnp.sum(da_lambda_q2[0] * da_lambda_k2[0]))
           + lam_init)
    lam_t = jnp.full((1, LANES), lam, F32)

    o_ab = DN_CH + DN_QK
    w0 = w_in[0]
    w_r = jnp.concatenate([w0[:, :o_ab], w0[:, o_ab + 2 * DN_HEADS:], w0[:, o_ab:o_ab + 2 * DN_HEADS],
                           jnp.zeros((D_MODEL, LANES - 2 * DN_HEADS), F32)], axis=1).astype(BF16)
    hp = jnp.zeros((SUBLANES, LANES), F32).at[0, :DN_HEADS].set(dn_a_log[0]).at[1, :DN_HEADS].set(dn_dt_bias[0])
    dn_nw = dn_norm_w[0].reshape(1, DN_D)
    da_nw = da_norm_w[0].reshape(1, DA_DV)
    p = dict(wq=peer_wq[0].astype(BF16), sk=peer_sub_keys[0].astype(BF16), u_tab=_pack_table(peer_u[0]),
             v_tab=_pack_table(peer_v[0]), ln2_g=ln2_g[0].reshape(1, D_MODEL), ln2_b=ln2_b[0].reshape(1, D_MODEL),
             alpha=alpha)
    w_out_b = w_out[0].astype(BF16)
    g1, b1 = ln1_g[0].reshape(1, D_MODEL), ln1_b[0].reshape(1, D_MODEL)

    xp = x_prompt.reshape(nb * t, D_MODEL)
    cos_p, sin_p = _rope_tables(jnp.tile(jnp.arange(t), nb))
    qkv, z, ab, q, kf, kb, vf, vb = _proj(xp, w_r, cos_p, sin_p)
    c = min(DN_CHUNK, t)
    dn, s_p, c_p = _deltanet(qkv, z, ab, dn_conv_w[0], hp, dn_nw, nb, t, c, c)
    da = _prompt_attention(q, kb, vb, lam_t, da_nw, nb, t, post)
    x1, x1t = _mix(xp, dn, da, w_out_b, g1, b1, alpha)
    y_p = _peer_and_norm(x1, x1t, p).reshape(nb, t, D_MODEL)
    k_p = kf.reshape(1, nb, t, DA_HEADS, 2 * DA_DQK)
    v_p = vf.reshape(1, nb, t, DA_HEADS, DA_DV)

    xs = x_sample.reshape(nsb * s_new, D_MODEL)
    cos_s, sin_s = _rope_tables(jnp.tile(past_len + jnp.arange(s_new), nsb))
    qkv, z, ab, q, kf, kb, vf, vb = _proj(xs, w_r, cos_s, sin_s)
    cs = SUBLANES * (-(-s_new // SUBLANES))
    pad = lambda a: jnp.pad(a.reshape(nsb, s_new, -1), ((0, 0), (0, cs - s_new), (0, 0))).reshape(nsb * cs, -1)
    dn, s_s, c_s = _deltanet(pad(qkv), pad(z), pad(ab), dn_conv_w[0], hp, dn_nw, nsb, cs, cs, s_new,
                             s0=state_dn[0], cb0=state_conv[0])
    dn = dn.reshape(nsb, cs, DN_QK)[:, :s_new].reshape(nsb * s_new, DN_QK)
    q5 = q.reshape(nsb, s_new, DA_HEADS, 2, DA_DQK).transpose(0, 2, 3, 1, 4)
    qw = (q5[:, :, :, :, None, :] * jnp.eye(2, dtype=BF16)[None, None, :, None, :, None]).reshape(
        nsb, DA_HEADS * 2 * s_new, 2 * DA_DQK)
    new_rows = s_new * DA_HEADS
    new_pad = SUBLANES * 2 * (-(-new_rows // (SUBLANES * 2)))
    padn = lambda a: jnp.pad(a.reshape(nsb, new_rows, LANES), ((0, 0), (0, new_pad - new_rows), (0, 0)))
    da = _paged_attention(page_table, lam_t, qw, cache_k[0].reshape(-1, page * DA_HEADS, LANES),
                          cache_v[0].reshape(-1, page * DA_HEADS, LANES), padn(kf), padn(vf), da_nw, s_new, post)
    da = da.reshape(nsb, DA_HEADS, s_new, DA_DV).transpose(0, 2, 1, 3).reshape(nsb * s_new, DA_W).astype(BF16)
    x1, x1t = _mix(xs, dn, da, w_out_b, g1, b1, alpha)
    y_s = _peer_and_norm(x1, x1t, p).reshape(nsb, s_new, D_MODEL)
    k_s = kf.reshape(1, nsb, s_new, DA_HEADS, 2 * DA_DQK)
    v_s = vf.reshape(1, nsb, s_new, DA_HEADS, DA_DV)

    return (y_p, y_s, k_p, v_p, k_s, v_s, s_p[None], c_p[None], s_s[None], c_s[None])
```

```python
import functools
import math

import jax
import jax.numpy as jnp
from jax import lax
from jax.experimental import pallas as pl
from jax.experimental.pallas import tpu as pltpu

F32 = jnp.float32
BF16 = jnp.bfloat16
I32 = jnp.int32
HI = lax.Precision.HIGHEST

LANES = 128
SUBLANES = 8
VMEM_PHYSICAL = 64 * 1024 * 1024

D_MODEL = 1024
DN_HEADS = 4
DN_D = 128
DN_CONV = 4
DN_QK = DN_HEADS * DN_D
DN_CH = 3 * DN_QK
DN_CHUNK = 64
DA_HEADS = 4
DA_DQK = 64
DA_DV = 128
DA_W = DA_HEADS * DA_DV
ROPE_THETA = 10000.0
PEER_HEADS = 8
PEER_DK = 256
N_KEYS = 128
PEER_TOPK = 16
N_PAIRS = PEER_HEADS * PEER_TOPK
HALF_EXPERTS = N_KEYS * N_KEYS // 2
PROJ_W = DN_CH + DN_QK + 3 * DA_W + LANES
NEG = -1e30
PAGES_PER_STEP = 16
DN_SEQS_PER_STEP = 4
U_ROW_SHIFT = 5
ROW_MASK = 0xFFF8
HI_HALF = -65536
PAIR_GROUP = 32


def _cparams(sem, vmem_bytes):
    return pltpu.CompilerParams(dimension_semantics=sem, vmem_limit_bytes=int(vmem_bytes))


def _row_tile(m, cap):
    t = cap
    while m % t:
        t //= 2
    assert t >= SUBLANES, (m, cap)
    return t


def _rope_cols(t, cos, sin_signed):
    lane = lax.broadcasted_iota(I32, (t.shape[0], LANES), 1)
    first = (lane & (DA_DQK - 1)) < (DA_DQK // 2)
    outs = []
    for h in range(DA_HEADS):
        th = t[:, h * LANES:(h + 1) * LANES]
        partner = jnp.where(first, pltpu.roll(th, LANES - DA_DQK // 2, axis=1), pltpu.roll(th, DA_DQK // 2, axis=1))
        outs.append(th * cos + partner * sin_signed)
    return jnp.concatenate(outs, axis=1)


def _proj_kernel(x_ref, w_ref, cos_ref, sin_ref, qkv_ref, z_ref, ab_ref, q_ref, kf_ref, kb_ref, vf_ref, vb_ref):
    xb = x_ref[...].astype(BF16)

    def mm(lo, hi):
        return jnp.dot(xb, w_ref[:, lo:hi], preferred_element_type=F32)

    qkv_ref[...] = mm(0, DN_CH)
    o = DN_CH
    z_ref[...] = mm(o, o + DN_QK)
    o += DN_QK
    cos = cos_ref[...]
    sin = sin_ref[...]
    q = _rope_cols(mm(o, o + DA_W), cos, sin)
    q_ref[...] = (q * (DA_DQK ** -0.5)).astype(BF16)
    o += DA_W
    k = _rope_cols(mm(o, o + DA_W), cos, sin)
    kf_ref[...] = k
    kb_ref[...] = k.astype(BF16)
    o += DA_W
    v = mm(o, o + DA_W)
    vf_ref[...] = v
    vb_ref[...] = v.astype(BF16)
    o += DA_W
    ab_ref[...] = mm(o, o + LANES)


def _proj(x2d, w_r, cos_t, sin_t):
    m = x2d.shape[0]
    tm = _row_tile(m, 256)
    row = lambda w: pl.BlockSpec((tm, w), lambda i: (i, 0))
    outs = [(DN_CH, F32), (DN_QK, F32), (LANES, F32), (DA_W, BF16), (DA_W, F32), (DA_W, BF16), (DA_W, F32), (DA_W, BF16)]
    return pl.pallas_call(
        _proj_kernel,
        grid=(m // tm,),
        in_specs=[row(D_MODEL), pl.BlockSpec((D_MODEL, PROJ_W), lambda i: (0, 0)), row(LANES), row(LANES)],
        out_specs=[row(w) for w, _ in outs],
        out_shape=[jax.ShapeDtypeStruct((m, w), d) for w, d in outs],
        compiler_params=_cparams(("parallel",), 48 << 20),
        name="proj",
    )(x2d, w_r, cos_t, sin_t)


def _dot(a, b, prec=HI):
    return jnp.dot(a, b, precision=prec, preferred_element_type=F32)


def _dot_nt(a, b, prec=HI):
    return lax.dot_general(a, b, (((1,), (1,)), ((), ())), precision=prec, preferred_element_type=F32)


def _dot1(a, b):
    return jnp.dot(a.astype(BF16), b.astype(BF16), preferred_element_type=F32)


def _dot1_nt(a, b):
    return lax.dot_general(a.astype(BF16), b.astype(BF16), (((1,), (1,)), ((), ())), preferred_element_type=F32)


def _dot1_tn(a, b):
    return lax.dot_general(a.astype(BF16), b.astype(BF16), (((0,), (0,)), ((), ())), preferred_element_type=F32)


def _split_bf16(a):
    hi = a.astype(BF16)
    return hi, (a - hi.astype(F32)).astype(BF16)


def _dot3(a, b):
    ah, al = _split_bf16(a)
    bh, bl = _split_bf16(b)
    return (jnp.dot(ah, bh, preferred_element_type=F32)
            + (jnp.dot(ah, bl, preferred_element_type=F32) + jnp.dot(al, bh, preferred_element_type=F32)))


def _unit_lower_inverses(mats, c):
    r = lax.broadcasted_iota(I32, (c, c), 0)
    s = lax.broadcasted_iota(I32, (c, c), 1)
    base = min(c, 16)
    lb = base.bit_length() - 1
    ns = [jnp.where((r >> lb) == (s >> lb), -a, 0.0) for a in mats]
    eye = jnp.where(r == s, 1.0, 0.0)
    ps = [eye + n for n in ns]
    for _ in range(lb - 1):
        ns = [_dot3(n, n) for n in ns]
        ps = [p + _dot3(p, n) for p, n in zip(ps, ns)]
    while base < c:
        lb = base.bit_length() - 1
        e_mask = ((r >> (lb + 1)) == (s >> (lb + 1))) & (((r >> lb) & 1) == 1) & (((s >> lb) & 1) == 0)
        pes = [_dot3(p, jnp.where(e_mask, a, 0.0)) for p, a in zip(ps, mats)]
        ps = [p - _dot3(pe, p) for p, pe in zip(ps, pes)]
        base *= 2
    return ps


def _silu(x):
    return x * (1.0 / (1.0 + jnp.exp(-x)))


def _dn_kernel(c, nvalid, has_state, bs, *refs):
    if has_state:
        (qkv_ref, z_ref, ab_ref, cw_ref, hp_ref, nw_ref, s0_ref, cb0_ref,
         out_ref, sfin_ref, cfin_ref, s_scr, xbuf) = refs
    else:
        (qkv_ref, z_ref, ab_ref, cw_ref, hp_ref, nw_ref,
         out_ref, sfin_ref, cfin_ref, s_scr, xbuf) = refs
    i = pl.program_id(1)
    last = pl.num_programs(1) - 1
    base = SUBLANES
    hist = DN_CONV - 1

    @pl.when(i == 0)
    def _():
        if has_state:
            s_scr[...] = s0_ref[...]
            xbuf[:, base - hist:base, :] = cb0_ref[...]
        else:
            s_scr[...] = jnp.zeros_like(s_scr)
            xbuf[:, 0:base, :] = jnp.zeros((bs, base, DN_CH), F32)

    rowi = lax.broadcasted_iota(I32, (c, LANES), 0)
    live = rowi < nvalid
    rr = lax.broadcasted_iota(I32, (c, c), 0)
    cc = lax.broadcasted_iota(I32, (c, c), 1)
    incl = rr >= cc
    strict = rr > cc
    tril = jnp.where(incl, 1.0, 0.0)
    sel = jnp.where(lax.broadcasted_iota(I32, (SUBLANES, LANES), 0) == lax.broadcasted_iota(I32, (SUBLANES, LANES), 1),
                    1.0, 0.0)

    acts, gcums, growss, beta_alls = [], [], [], []
    for b in range(bs):
        xbuf[b, base:base + c, :] = qkv_ref[b]
        acc = xbuf[b, base - hist:base - hist + c, :] * cw_ref[0:1, :]
        for w in range(1, DN_CONV):
            acc = acc + xbuf[b, base - hist + w:base - hist + w + c, :] * cw_ref[w:w + 1, :]
        acts.append(_silu(acc))
        ab = ab_ref[b]
        sp_in = ab + hp_ref[1:2, :]
        softplus = jnp.maximum(sp_in, 0.0) + jnp.log(1.0 + jnp.exp(-jnp.abs(sp_in)))
        g_all = jnp.where(live, -jnp.exp(hp_ref[0:1, :]) * softplus, 0.0)
        beta_alls.append(jnp.where(live, 1.0 / (1.0 + jnp.exp(-ab)), 0.0))
        gcum = _dot(tril, g_all)
        gcums.append(gcum)
        growss.append(_dot_nt(sel, gcum))

    @pl.when(i == last)
    def _():
        cfin_ref[...] = xbuf[:, base + nvalid - hist:base + nvalid, :]

    xbuf[:, base - hist:base, :] = xbuf[:, base + c - hist:base + c, :]

    chains = [(b, h) for b in range(bs) for h in range(DN_HEADS)]
    n = range(len(chains))
    qs, ks, vs = [], [], []
    for b, h in chains:
        qh = acts[b][:, h * DN_D:(h + 1) * DN_D]
        kh = acts[b][:, DN_QK + h * DN_D:DN_QK + (h + 1) * DN_D]
        qs.append(qh * lax.rsqrt(jnp.sum(qh * qh, axis=-1, keepdims=True) + 1e-6) * (DN_D ** -0.5))
        ks.append(kh * lax.rsqrt(jnp.sum(kh * kh, axis=-1, keepdims=True) + 1e-6))
        vs.append(acts[b][:, 2 * DN_QK + h * DN_D:2 * DN_QK + (h + 1) * DN_D])
    betas = [beta_alls[b][:, DN_HEADS + h:DN_HEADS + h + 1] for b, h in chains]
    gcols = [gcums[b][:, h:h + 1] for b, h in chains]
    glasts = [gcums[b][c - 1:c, h:h + 1] for b, h in chains]
    decays = [jnp.exp(jnp.where(incl, gcols[x] - growss[b][h:h + 1, :], NEG))
              for x, (b, h) in enumerate(chains)]
    egs = [jnp.exp(g) for g in gcols]
    kbs = [ks[x] * betas[x] for x in n]
    amats = [jnp.where(strict, _dot1_nt(kbs[x], ks[x]) * decays[x], 0.0) for x in n]
    tinvs = _unit_lower_inverses(amats, c)
    us = [_dot3(tinvs[x], vs[x] * betas[x]) for x in n]
    ws = [_dot3(tinvs[x], kbs[x] * egs[x]) for x in n]
    aqks = [_dot1_nt(qs[x], ks[x]) * decays[x] for x in n]
    ss = [s_scr[b, h] for b, h in chains]
    v_news = [us[x] - _dot1(ws[x], ss[x]) for x in n]
    outs = [_dot1(qs[x] * egs[x], ss[x]) + _dot1(aqks[x], v_news[x]) for x in n]
    for x, (b, h) in enumerate(chains):
        s_scr[b, h] = ss[x] * jnp.exp(glasts[x]) + _dot1_tn(ks[x] * jnp.exp(glasts[x] - gcols[x]), v_news[x])
    for x, (b, h) in enumerate(chains):
        o = outs[x]
        zh = z_ref[b, :, h * DN_D:(h + 1) * DN_D]
        on = o * lax.rsqrt(jnp.mean(o * o, axis=-1, keepdims=True) + 1e-6) * nw_ref[...]
        out_ref[b, :, h * DN_D:(h + 1) * DN_D] = (on * _silu(zh)).astype(out_ref.dtype)

    @pl.when(i == last)
    def _():
        sfin_ref[...] = s_scr[...]


def _deltanet(qkv, z, ab, conv_w, hp, norm_w, c, nvalid, s0=None, cb0=None):
    nb, t_rows, _ = qkv.shape
    nc = t_rows // c
    bs = math.gcd(nb, DN_SEQS_PER_STEP)
    has_state = s0 is not None
    row = lambda w: pl.BlockSpec((bs, c, w), lambda b, i: (b, i, 0))
    full = lambda shape: pl.BlockSpec(shape, lambda b, i: (0,) * len(shape))
    state = pl.BlockSpec((bs, DN_HEADS, DN_D, DN_D), lambda b, i: (b, 0, 0, 0))
    cbuf = pl.BlockSpec((bs, DN_CONV - 1, DN_CH), lambda b, i: (b, 0, 0))
    in_specs = [row(DN_CH), row(DN_QK), row(LANES), full((DN_CONV, DN_CH)), full((SUBLANES, LANES)), full((1, DN_D))]
    args = [qkv, z, ab, conv_w, hp, norm_w]
    if has_state:
        in_specs += [state, cbuf]
        args += [s0, cb0]
    return pl.pallas_call(
        functools.partial(_dn_kernel, c, nvalid, has_state, bs),
        grid=(nb // bs, nc),
        in_specs=in_specs,
        out_specs=[row(DN_QK), state, cbuf],
        out_shape=[jax.ShapeDtypeStruct((nb, t_rows, DN_QK), BF16),
                   jax.ShapeDtypeStruct((nb, DN_HEADS, DN_D, DN_D), F32),
                   jax.ShapeDtypeStruct((nb, DN_CONV - 1, DN_CH), F32)],
        scratch_shapes=[pltpu.VMEM((bs, DN_HEADS, DN_D, DN_D), F32), pltpu.VMEM((bs, c + SUBLANES, DN_CH), F32)],
        compiler_params=_cparams(("parallel", "arbitrary"), 48 << 20),
        name="deltanet",
    )(*args)


def _da_finish(o1, o2, lam, nw, post):
    o = o1 - lam * o2
    return o * lax.rsqrt(jnp.mean(o * o, axis=-1, keepdims=True) + 1e-6) * nw * post


def _attn_kernel(tq, post, qi_ref, kj_ref, lam_ref, q_ref, k_ref, v_ref, nw_ref, out_ref, m_scr, acc_scr):
    i = qi_ref[pl.program_id(2)]
    j = kj_ref[pl.program_id(2)]
    nt = tq // LANES

    @pl.when(j == 0)
    def _():
        m_scr[...] = jnp.full_like(m_scr, NEG)
        acc_scr[...] = jnp.zeros_like(acc_scr)

    def update(masked):
        q = q_ref[...]
        k = k_ref[...]
        v1 = jnp.concatenate([v_ref[...], jnp.ones((tq, LANES), BF16)], axis=1)
        if masked:
            keep = lax.broadcasted_iota(I32, (tq, tq), 0) >= lax.broadcasted_iota(I32, (tq, tq), 1)
        halves = range(2)
        ss = [_dot_nt(q[:, c * DA_DQK:(c + 1) * DA_DQK], k[:, c * DA_DQK:(c + 1) * DA_DQK], prec=None)
              for c in halves]
        if masked:
            ss = [jnp.where(keep, s, NEG) for s in ss]
        m_olds = [m_scr[c] for c in halves]
        m_news = [jnp.maximum(m_olds[c], jnp.max(ss[c], axis=-1, keepdims=True)) for c in halves]
        ps = [jnp.concatenate([jnp.exp(ss[c][:, a * LANES:(a + 1) * LANES] - m_news[c]) for a in range(nt)], axis=1)
              for c in halves]
        alphas = [jnp.exp(m_olds[c] - m_news[c]) for c in halves]
        pvs = [jnp.dot(ps[c].astype(BF16), v1, preferred_element_type=F32) for c in halves]
        for c in halves:
            acc_scr[c] = jnp.concatenate([alphas[c], alphas[c]], axis=1) * acc_scr[c] + pvs[c]
            m_scr[c] = m_news[c]

    @pl.when(j < i)
    def _():
        update(False)

    @pl.when(j == i)
    def _():
        update(True)
        o1 = acc_scr[0, :, 0:LANES] / acc_scr[0, :, LANES:]
        o2 = acc_scr[1, :, 0:LANES] / acc_scr[1, :, LANES:]
        out_ref[...] = _da_finish(o1, o2, lam_ref[0:1, 0:1], nw_ref[...], post).astype(out_ref.dtype)


def _prompt_attention(q, k, v, lam_t, nw, nb, t, post):
    tq = _row_tile(t, 512)
    nq = t // tq
    pairs = [(i, j) for i in range(nq) for j in range(i + 1)]
    qi = jnp.asarray([p[0] for p in pairs], I32)
    kj = jnp.asarray([p[1] for p in pairs], I32)
    q_spec = pl.BlockSpec((tq, LANES), lambda b, h, s, qi, kj: (b * nq + qi[s], h))
    kv_spec = pl.BlockSpec((tq, LANES), lambda b, h, s, qi, kj: (b * nq + kj[s], h))
    const = pl.BlockSpec((1, LANES), lambda b, h, s, qi, kj: (0, 0))
    grid_spec = pltpu.PrefetchScalarGridSpec(
        num_scalar_prefetch=2,
        grid=(nb, DA_HEADS, len(pairs)),
        in_specs=[const, q_spec, kv_spec, kv_spec, const],
        out_specs=q_spec,
        scratch_shapes=[pltpu.VMEM((2, tq, LANES), F32), pltpu.VMEM((2, tq, 2 * LANES), F32)],
    )
    return pl.pallas_call(
        functools.partial(_attn_kernel, tq, post),
        grid_spec=grid_spec,
        out_shape=jax.ShapeDtypeStruct((nb * t, DA_W), BF16),
        compiler_params=_cparams(("parallel", "parallel", "arbitrary"), 32 << 20),
        name="prompt_attention",
    )(qi, kj, lam_t, q, k, v, nw)


def _paged_kernel(s_new, post, npg, pt_ref, lam_ref, qw_ref, *refs):
    k_refs, v_refs = refs[:npg], refs[npg:2 * npg]
    kn_ref, vn_ref, nw_ref, out_ref, m_scr, l_scr, acc_scr = refs[2 * npg:]
    p_id = pl.program_id(1)
    nrow = DA_HEADS * 2 * s_new
    qw = qw_ref[0]

    @pl.when(p_id == 0)
    def _():
        m_scr[...] = jnp.full_like(m_scr, NEG)
        l_scr[...] = jnp.zeros_like(l_scr)
        acc_scr[...] = jnp.zeros_like(acc_scr)

    def keep_mask(nk, causal):
        row = lax.broadcasted_iota(I32, (nrow, nk), 0)
        col = lax.broadcasted_iota(I32, (nrow, nk), 1)
        keep = (col & (DA_HEADS - 1)) == (row >> ((2 * s_new).bit_length() - 1))
        if causal:
            keep = keep & ((col >> (DA_HEADS.bit_length() - 1)) <= (row & (s_new - 1)))
        return keep

    def accumulate(kblks, vblks, keep):
        ss = [jnp.where(keep, _dot_nt(qw, kb.astype(BF16), prec=None), NEG) for kb in kblks]
        smax = ss[0]
        for s in ss[1:]:
            smax = jnp.maximum(smax, s)
        m_old = m_scr[...]
        m_new = jnp.maximum(m_old, jnp.max(smax, axis=-1, keepdims=True))
        ps = [jnp.where(keep, jnp.exp(s - m_new), 0.0) for s in ss]
        psum = ps[0]
        for p in ps[1:]:
            psum = psum + p
        alpha = jnp.exp(m_old - m_new)
        l_scr[...] = alpha * l_scr[...] + jnp.sum(psum, axis=-1, keepdims=True)
        pv = jnp.dot(ps[0].astype(BF16), vblks[0].astype(BF16), preferred_element_type=F32)
        for p, vb in zip(ps[1:], vblks[1:]):
            pv = pv + jnp.dot(p.astype(BF16), vb.astype(BF16), preferred_element_type=F32)
        acc_scr[...] = alpha * acc_scr[...] + pv
        m_scr[...] = m_new

    accumulate([r[0] for r in k_refs], [r[0] for r in v_refs], keep_mask(k_refs[0].shape[1], False))

    @pl.when(p_id == pl.num_programs(1) - 1)
    def _():
        accumulate([kn_ref[0]], [vn_ref[0]], keep_mask(kn_ref.shape[1], True))
        o = acc_scr[...] / l_scr[...]
        lam = lam_ref[0:1, 0:1]
        for h in range(DA_HEADS):
            o1 = o[h * 2 * s_new:h * 2 * s_new + s_new]
            o2 = o[h * 2 * s_new + s_new:(h + 1) * 2 * s_new]
            out_ref[0, h * s_new:(h + 1) * s_new, :] = _da_finish(o1, o2, lam, nw_ref[...], post)


def _paged_attention(page_table, lam_t, qw, cache_k, cache_v, k_new, v_new, nw, s_new, post):
    nb, n_pages = page_table.shape
    page_rows = cache_k.shape[1]
    nrow = DA_HEADS * 2 * s_new
    new_rows = k_new.shape[1]
    npg = math.gcd(n_pages, PAGES_PER_STEP)

    def page_spec(i):
        return pl.BlockSpec((1, page_rows, LANES), lambda b, p, pt: (pt[b, p * npg + i], 0, 0))

    grid_spec = pltpu.PrefetchScalarGridSpec(
        num_scalar_prefetch=1,
        grid=(nb, n_pages // npg),
        in_specs=[pl.BlockSpec((1, LANES), lambda b, p, pt: (0, 0)),
                  pl.BlockSpec((1, nrow, LANES), lambda b, p, pt: (b, 0, 0))]
                 + [page_spec(i) for i in range(npg)] * 2
                 + [pl.BlockSpec((1, new_rows, LANES), lambda b, p, pt: (b, 0, 0)),
                    pl.BlockSpec((1, new_rows, LANES), lambda b, p, pt: (b, 0, 0)),
                    pl.BlockSpec((1, LANES), lambda b, p, pt: (0, 0))],
        out_specs=pl.BlockSpec((1, DA_HEADS * s_new, LANES), lambda b, p, pt: (b, 0, 0)),
        scratch_shapes=[pltpu.VMEM((nrow, 1), F32), pltpu.VMEM((nrow, 1), F32), pltpu.VMEM((nrow, LANES), F32)],
    )
    return pl.pallas_call(
        functools.partial(_paged_kernel, s_new, post, npg),
        grid_spec=grid_spec,
        out_shape=jax.ShapeDtypeStruct((nb, DA_HEADS * s_new, LANES), F32),
        compiler_params=_cparams(("parallel", "arbitrary"), 32 << 20),
        name="paged_attention",
    )(page_table, lam_t, qw, *([cache_k] * npg), *([cache_v] * npg), k_new, v_new, nw)


def _layer_norm(y, g, b):
    mu = jnp.mean(y, axis=-1, keepdims=True)
    d = y - mu
    var = jnp.mean(d * d, axis=-1, keepdims=True)
    return d * lax.rsqrt(var + 1e-5) * g + b


def _mix_kernel(alpha, x_ref, dn_ref, da_ref, w_ref, g_ref, b_ref, x1_ref, x1t_ref):
    mix = (jnp.dot(dn_ref[...], w_ref[0:DN_QK, :], preferred_element_type=F32)
           + jnp.dot(da_ref[...], w_ref[DN_QK:, :], preferred_element_type=F32))
    x1 = _layer_norm(alpha * x_ref[...] + mix, g_ref[...], b_ref[...])
    x1_ref[...] = x1
    for s in range(SUBLANES):
        x1t_ref[:, s, :] = x1[:, s * LANES:(s + 1) * LANES]


def _mix(x2d, dn, da, w_out, g, b, alpha):
    m = x2d.shape[0]
    tm = _row_tile(m, 256)
    row = lambda w: pl.BlockSpec((tm, w), lambda i: (i, 0))
    full = lambda shape: pl.BlockSpec(shape, lambda i: (0,) * len(shape))
    return pl.pallas_call(
        functools.partial(_mix_kernel, alpha),
        grid=(m // tm,),
        in_specs=[row(D_MODEL), row(DN_QK), row(DA_W), full((D_MODEL, D_MODEL)), full((1, D_MODEL)), full((1, D_MODEL))],
        out_specs=[row(D_MODEL), pl.BlockSpec((tm, SUBLANES, LANES), lambda i: (i, 0, 0))],
        out_shape=[jax.ShapeDtypeStruct((m, D_MODEL), F32), jax.ShapeDtypeStruct((m, SUBLANES, LANES), F32)],
        compiler_params=_cparams(("parallel",), 32 << 20),
        name="mix_ln1",
    )(x2d, dn, da, w_out, g, b)


def _top16_rows(vals, payload=None):
    r = vals.shape[0]
    iota = lax.broadcasted_iota(I32, vals.shape, 0)
    tops, idxs = [], []
    for _ in range(PEER_TOPK):
        m = jnp.max(vals, axis=0, keepdims=True)
        am = jnp.min(jnp.where(vals == m, iota, r), axis=0, keepdims=True)
        hit = iota == am
        tops.append(m)
        idxs.append(am if payload is None else jnp.sum(jnp.where(hit, payload, 0), axis=0, keepdims=True))
        vals = jnp.where(hit, -jnp.inf, vals)
    return tops, idxs


def _route_kernel(x_ref, wq_ref, sk_ref, code_ref, gate_ref):
    xb = x_ref[...].astype(BF16)
    half = PEER_DK // 2
    gates, codes = [], []
    for h in range(PEER_HEADS):
        picks = []
        for c in range(2):
            col = (h * 2 + c) * half
            q = jnp.dot(xb, wq_ref[:, col:col + half], preferred_element_type=F32)
            st = _dot_nt(sk_ref[c], q.astype(BF16), prec=None)
            picks.append(_top16_rows(st))
        (s1, i1), (s2, i2) = picks
        s2m = jnp.concatenate(s2, axis=0)
        i2m = jnp.concatenate(i2, axis=0)
        sub8 = lax.broadcasted_iota(I32, (SUBLANES, s2m.shape[1]), 0)
        cand = [s1[0] + s2m]
        cidx = [i1[0] * N_KEYS + i2m]
        for a in range(1, SUBLANES):
            cand.append(jnp.where(sub8 < PEER_TOPK // (a + 1), s1[a] + s2m[:SUBLANES], -jnp.inf))
            cidx.append(i1[a] * N_KEYS + i2m[:SUBLANES])
        cand.append(jnp.concatenate(s1[SUBLANES:], axis=0) + s2[0])
        cidx.append(jnp.concatenate(i1[SUBLANES:], axis=0) * N_KEYS + i2[0])
        top, eidx = _top16_rows(jnp.concatenate(cand, axis=0), jnp.concatenate(cidx, axis=0))
        ex = [jnp.exp(t - top[0]) for t in top]
        den = ex[0]
        for e in ex[1:]:
            den = den + e
        gates.append(jnp.concatenate(ex, axis=0) / den)
        e = jnp.concatenate(eidx, axis=0)
        codes.append(((e & (HALF_EXPERTS - 1)) << (U_ROW_SHIFT + 3)) | (16 - ((e >> 13) << 4)))
    gate_ref[...] = jnp.concatenate(gates, axis=0).T
    code_ref[...] = jnp.concatenate(codes, axis=0).T


def _route(x1, wq, sk):
    m = x1.shape[0]
    tm = _row_tile(m, 256)
    full = lambda shape: pl.BlockSpec(shape, lambda i: (0,) * len(shape))
    row = pl.BlockSpec((tm, N_PAIRS), lambda i: (i, 0))
    return pl.pallas_call(
        _route_kernel,
        grid=(m // tm,),
        in_specs=[pl.BlockSpec((tm, D_MODEL), lambda i: (i, 0)), full(wq.shape), full(sk.shape)],
        out_specs=[row, row],
        out_shape=[jax.ShapeDtypeStruct((m, N_PAIRS), I32), jax.ShapeDtypeStruct((m, N_PAIRS), F32)],
        compiler_params=_cparams(("parallel",), 32 << 20),
        name="peer_route",
    )(x1, wq, sk)


def _expert_row(tab_ref, row, shift_v):
    word = tab_ref[pl.ds(pl.multiple_of(row, SUBLANES), SUBLANES), :]
    return pltpu.bitcast((word << shift_v) & HI_HALF, F32)


def _splat(s):
    return jnp.full((SUBLANES, LANES), s, I32)


def _fold_pair(a, b, s, sub):
    m = (sub & s) == 0
    return jnp.where(m, a, b) + pltpu.roll(jnp.where(m, b, a), s, axis=0)


def _sublane_sums(p, sub):
    c = [_fold_pair(p[2 * i], p[2 * i + 1], 1, sub) for i in range(4)]
    d = [_fold_pair(c[0], c[1], 2, sub), _fold_pair(c[2], c[3], 2, sub)]
    return _fold_pair(d[0], d[1], 4, sub)


def _peer_u_kernel(tb, code_ref, x_ref, gate_ref, codev_ref, tab_ref, wc_ref, part_scr, h_scr):
    sub = lax.broadcasted_iota(I32, (SUBLANES, LANES), 0)
    ones = jnp.ones((SUBLANES, LANES), BF16)
    ngrp = N_PAIRS // PAIR_GROUP
    oct_tokens = SUBLANES

    def octet(o, carry):
        def step(it, carry):
            tt = lax.shift_right_logical(it, ngrp.bit_length() - 1)
            g = it & (ngrp - 1)
            t = o * oct_tokens + tt
            xv = x_ref[t]
            for j in range(PAIR_GROUP // SUBLANES):
                k0 = g * PAIR_GROUP + j * SUBLANES
                codes = [code_ref[t * N_PAIRS + k0 + i] for i in range(SUBLANES)]
                prods = [_expert_row(tab_ref, lax.shift_right_logical(c, U_ROW_SHIFT), _splat(c) & 31) * xv
                         for c in codes]
                part_scr[pl.ds(pl.multiple_of(tt * N_PAIRS + k0, SUBLANES), SUBLANES), :] = _sublane_sums(prods, sub)
            return carry

        lax.fori_loop(0, oct_tokens * ngrp, step, 0)
        hi, lo = _split_bf16(part_scr[...])
        tot = (lax.dot_general(ones, hi, (((1,), (1,)), ((), ())), preferred_element_type=F32)
               + lax.dot_general(ones, lo, (((1,), (1,)), ((), ())), preferred_element_type=F32))
        tile = tot[:, 0:N_PAIRS]
        for j in range(1, oct_tokens):
            tile = jnp.where(sub == j, tot[:, j * N_PAIRS:(j + 1) * N_PAIRS], tile)
        h_scr[pl.ds(pl.multiple_of(o * oct_tokens, SUBLANES), SUBLANES), :] = tile
        return carry

    lax.fori_loop(0, tb // oct_tokens, octet, 0)
    h = h_scr[...]
    w = gate_ref[...] * (0.5 * h * (1.0 + lax.erf(h * (0.5 ** 0.5))))
    cv = codev_ref[...]
    code_v = lax.shift_right_logical(cv, U_ROW_SHIFT) | ((cv >> 4) & 1)
    wc_ref[...] = pltpu.bitcast(w.astype(BF16).astype(F32), I32) | code_v


def _peer_v_kernel(tb, wc_ref, tab_ref, out_ref):
    nacc = 4

    def token(t, carry):
        def group(g, accs):
            accs = list(accs)
            for i in range(PAIR_GROUP):
                wc = wc_ref[t * N_PAIRS + g * PAIR_GROUP + i]
                wc_v = _splat(wc)
                weight = pltpu.bitcast(wc_v & HI_HALF, F32)
                row = _expert_row(tab_ref, wc & ROW_MASK, (wc_v & 1) << 4)
                accs[i % nacc] = accs[i % nacc] + row * weight
            return tuple(accs)

        accs = lax.fori_loop(0, N_PAIRS // PAIR_GROUP, group,
                             tuple(jnp.zeros((SUBLANES, LANES), F32) for _ in range(nacc)))
        out_ref[t] = (accs[0] + accs[1]) + (accs[2] + accs[3])
        return carry

    lax.fori_loop(0, tb, token, 0)


def _peer_tb(m):
    return LANES if m % LANES == 0 else m


def _peer_u(code, x1t, gate, tab):
    m = x1t.shape[0]
    tb = _peer_tb(m)
    row = pl.BlockSpec((tb, N_PAIRS), lambda i: (i, 0))
    return pl.pallas_call(
        functools.partial(_peer_u_kernel, tb),
        grid=(m // tb,),
        in_specs=[pl.BlockSpec((tb * N_PAIRS,), lambda i: (i,), memory_space=pltpu.SMEM),
                  pl.BlockSpec((tb, SUBLANES, LANES), lambda i: (i, 0, 0)), row, row,
                  pl.BlockSpec(tab.shape, lambda i: (0, 0))],
        out_specs=row,
        out_shape=jax.ShapeDtypeStruct((m, N_PAIRS), I32),
        scratch_shapes=[pltpu.VMEM((SUBLANES * N_PAIRS, LANES), F32), pltpu.VMEM((tb, N_PAIRS), F32)],
        compiler_params=_cparams(("arbitrary",), VMEM_PHYSICAL - (8 << 20)),
        name="peer_u",
    )(code.reshape(-1), x1t, gate, code, tab)


def _peer_v(wc, tab):
    m = wc.shape[0]
    tb = _peer_tb(m)
    return pl.pallas_call(
        functools.partial(_peer_v_kernel, tb),
        grid=(m // tb,),
        in_specs=[pl.BlockSpec((tb * N_PAIRS,), lambda i: (i,), memory_space=pltpu.SMEM),
                  pl.BlockSpec(tab.shape, lambda i: (0, 0))],
        out_specs=pl.BlockSpec((tb, SUBLANES, LANES), lambda i: (i, 0, 0)),
        out_shape=jax.ShapeDtypeStruct((m, SUBLANES, LANES), F32),
        compiler_params=_cparams(("arbitrary",), VMEM_PHYSICAL - (8 << 20)),
        name="peer_v",
    )(wc.reshape(-1), tab)


def _ln2_kernel(alpha, x1_ref, ffn_ref, g_ref, b_ref, y_ref):
    ffn = jnp.concatenate([ffn_ref[:, s, :] for s in range(SUBLANES)], axis=1)
    y_ref[...] = _layer_norm(alpha * x1_ref[...] + ffn, g_ref[...], b_ref[...])


def _ln2(x1, ffn_t, g, b, alpha):
    m = x1.shape[0]
    tm = _row_tile(m, 256)
    row = pl.BlockSpec((tm, D_MODEL), lambda i: (i, 0))
    full = pl.BlockSpec((1, D_MODEL), lambda i: (0, 0))
    return pl.pallas_call(
        functools.partial(_ln2_kernel, alpha),
        grid=(m // tm,),
        in_specs=[row, pl.BlockSpec((tm, SUBLANES, LANES), lambda i: (i, 0, 0)), full, full],
        out_specs=row,
        out_shape=jax.ShapeDtypeStruct((m, D_MODEL), F32),
        compiler_params=_cparams(("parallel",), 32 << 20),
        name="ln2",
    )(x1, ffn_t, g, b)


def _pack_table(tab):
    bits = lax.bitcast_convert_type(tab.astype(BF16), jnp.uint16).astype(jnp.uint32)
    word = bits[:HALF_EXPERTS] | (bits[HALF_EXPERTS:] << 16)
    return lax.bitcast_convert_type(word, I32).reshape(HALF_EXPERTS * SUBLANES, LANES)


def _rope_tables(pos):
    half = DA_DQK // 2
    inv_freq = ROPE_THETA ** (-jnp.arange(half, dtype=F32) * 2.0 / DA_DQK)
    ang = pos.astype(F32)[:, None] * inv_freq[None, :]
    cos, sin = jnp.cos(ang), jnp.sin(ang)
    return jnp.tile(cos, (1, LANES // half)), jnp.tile(jnp.concatenate([-sin, sin], axis=1), (1, LANES // DA_DQK))


def _peer_and_norm(x1, x1t, p):
    code, gate = _route(x1, p["wq"], p["sk"])
    wc = _peer_u(code, x1t, gate, p["u_tab"])
    ffn_t = _peer_v(wc, p["v_tab"])
    return _ln2(x1, ffn_t, p["ln2_g"], p["ln2_b"], p["alpha"])


def kernel(x_prompt, x_sample, cache_k, cache_v, state_dn, state_conv, page_table, w_in, dn_conv_w, dn_a_log,
           dn_dt_bias, dn_norm_w, da_lambda_q1, da_lambda_k1, da_lambda_q2, da_lambda_k2, da_norm_w, w_out,
           ln1_g, ln1_b, peer_wq, peer_sub_keys, peer_u, peer_v, ln2_g, ln2_b):
    depth = w_in.shape[0]
    assert depth == 1, "single-layer trunk"
    nb, t, _ = x_prompt.shape
    nsb, s_new, _ = x_sample.shape
    assert s_new & (s_new - 1) == 0, "sample length must be a power of two"
    n_pages, page = page_table.shape[1], cache_k.shape[2]
    past_len = n_pages * page
    alpha = (2 * depth) ** 0.25
    lam_init = 0.8 - 0.6 * math.exp(-0.3 * 0)
    post = 1.0 - lam_init
    lam = (jnp.exp(jnp.sum(da_lambda_q1[0] * da_lambda_k1[0])) - jnp.exp(jnp.sum(da_lambda_q2[0] * da_lambda_k2[0]))
           + lam_init)
    lam_t = jnp.full((1, LANES), lam, F32)

    o_ab = DN_CH + DN_QK
    w0 = w_in[0]
    w_r = jnp.concatenate([w0[:, :o_ab], w0[:, o_ab + 2 * DN_HEADS:], w0[:, o_ab:o_ab + 2 * DN_HEADS],
                           jnp.zeros((D_MODEL, LANES - 2 * DN_HEADS), F32)], axis=1).astype(BF16)
    hp = jnp.zeros((SUBLANES, LANES), F32).at[0, :DN_HEADS].set(dn_a_log[0]).at[1, :DN_HEADS].set(dn_dt_bias[0])
    dn_nw = dn_norm_w[0].reshape(1, DN_D)
    da_nw = da_norm_w[0].reshape(1, DA_DV)
    p = dict(wq=peer_wq[0].astype(BF16), sk=peer_sub_keys[0].astype(BF16), u_tab=_pack_table(peer_u[0]),
             v_tab=_pack_table(peer_v[0]), ln2_g=ln2_g[0].reshape(1, D_MODEL), ln2_b=ln2_b[0].reshape(1, D_MODEL),
             alpha=alpha)
    w_out_b = w_out[0].astype(BF16)
    g1, b1 = ln1_g[0].reshape(1, D_MODEL), ln1_b[0].reshape(1, D_MODEL)

    xp = x_prompt.reshape(nb * t, D_MODEL)
    cos_p, sin_p = _rope_tables(jnp.tile(jnp.arange(t), nb))
    qkv, z, ab, q, kf, kb, vf, vb = _proj(xp, w_r, cos_p, sin_p)
    c = min(DN_CHUNK, t)
    seq = lambda a: a.reshape(nb, t, a.shape[-1])
    dn, s_p, c_p = _deltanet(seq(qkv), seq(z), seq(ab), dn_conv_w[0], hp, dn_nw, c, c)
    da = _prompt_attention(q, kb, vb, lam_t, da_nw, nb, t, post)
    x1, x1t = _mix(xp, dn.reshape(nb * t, DN_QK), da, w_out_b, g1, b1, alpha)
    y_p = _peer_and_norm(x1, x1t, p).reshape(nb, t, D_MODEL)
    k_p = kf.reshape(1, nb, t, DA_HEADS, 2 * DA_DQK)
    v_p = vf.reshape(1, nb, t, DA_HEADS, DA_DV)

    xs = x_sample.reshape(nsb * s_new, D_MODEL)
    cos_s, sin_s = _rope_tables(jnp.tile(past_len + jnp.arange(s_new), nsb))
    qkv, z, ab, q, kf, kb, vf, vb = _proj(xs, w_r, cos_s, sin_s)
    cs = SUBLANES * (-(-s_new // SUBLANES))
    pad = lambda a: jnp.pad(a.reshape(nsb, s_new, -1), ((0, 0), (0, cs - s_new), (0, 0)))
    dn, s_s, c_s = _deltanet(pad(qkv), pad(z), pad(ab), dn_conv_w[0], hp, dn_nw, cs, s_new,
                             s0=state_dn[0], cb0=state_conv[0])
    dn = dn[:, :s_new].reshape(nsb * s_new, DN_QK)
    q5 = q.reshape(nsb, s_new, DA_HEADS, 2, DA_DQK).transpose(0, 2, 3, 1, 4)
    qw = (q5[:, :, :, :, None, :] * jnp.eye(2, dtype=BF16)[None, None, :, None, :, None]).reshape(
        nsb, DA_HEADS * 2 * s_new, 2 * DA_DQK)
    new_rows = s_new * DA_HEADS
    new_pad = SUBLANES * 2 * (-(-new_rows // (SUBLANES * 2)))
    padn = lambda a: jnp.pad(a.reshape(nsb, new_rows, LANES), ((0, 0), (0, new_pad - new_rows), (0, 0)))
    da = _paged_attention(page_table, lam_t, qw, cache_k[0].reshape(-1, page * DA_HEADS, LANES),
                          cache_v[0].reshape(-1, page * DA_HEADS, LANES), padn(kf), padn(vf), da_nw, s_new, post)
    da = da.reshape(nsb, DA_HEADS, s_new, DA_DV).transpose(0, 2, 1, 3).reshape(nsb * s_new, DA_W).astype(BF16)
    x1, x1t = _mix(xs, dn, da, w_out_b, g1, b1, alpha)
    y_s = _peer_and_norm(x1, x1t, p).reshape(nsb, s_new, D_MODEL)
    k_s = kf.reshape(1, nsb, s_new, DA_HEADS, 2 * DA_DQK)
    v_s = vf.reshape(1, nsb, s_new, DA_HEADS, DA_DV)

    return (y_p, y_s, k_p, v_p, k_s, v_s, s_p[None], c_p[None], s_s[None], c_s[None])
```

```python
import functools
import math

import jax
import jax.numpy as jnp
from jax import lax
from jax.experimental import pallas as pl
from jax.experimental.pallas import tpu as pltpu

F32 = jnp.float32
BF16 = jnp.bfloat16
I32 = jnp.int32
HI = lax.Precision.HIGHEST

LANES = 128
SUBLANES = 8
VMEM_PHYSICAL = 64 * 1024 * 1024

D_MODEL = 1024
DN_HEADS = 4
DN_D = 128
DN_CONV = 4
DN_QK = DN_HEADS * DN_D
DN_CH = 3 * DN_QK
DN_CHUNK = 64
DA_HEADS = 4
DA_DQK = 64
DA_DV = 128
DA_W = DA_HEADS * DA_DV
ROPE_THETA = 10000.0
PEER_HEADS = 8
PEER_DK = 256
N_KEYS = 128
PEER_TOPK = 16
N_PAIRS = PEER_HEADS * PEER_TOPK
HALF_EXPERTS = N_KEYS * N_KEYS // 2
PROJ_W = DN_CH + DN_QK + 3 * DA_W + LANES
NEG = -1e30
PAGES_PER_STEP = 16
DN_SEQS_PER_STEP = 4
LANE_SUM_TOKENS = 32
U_ROW_SHIFT = 5
ROW_MASK = 0xFFF8
HI_HALF = -65536


def _cparams(sem, vmem_bytes):
    return pltpu.CompilerParams(dimension_semantics=sem, vmem_limit_bytes=int(vmem_bytes))


def _row_tile(m, cap):
    t = cap
    while m % t:
        t //= 2
    assert t >= SUBLANES, (m, cap)
    return t


def _rope_cols(t, cos, sin_signed):
    lane = lax.broadcasted_iota(I32, (t.shape[0], LANES), 1)
    first = (lane & (DA_DQK - 1)) < (DA_DQK // 2)
    outs = []
    for h in range(DA_HEADS):
        th = t[:, h * LANES:(h + 1) * LANES]
        partner = jnp.where(first, pltpu.roll(th, LANES - DA_DQK // 2, axis=1), pltpu.roll(th, DA_DQK // 2, axis=1))
        outs.append(th * cos + partner * sin_signed)
    return jnp.concatenate(outs, axis=1)


def _proj_kernel(x_ref, w_ref, cos_ref, sin_ref, qkv_ref, z_ref, ab_ref, q_ref, kf_ref, kb_ref, vf_ref, vb_ref):
    xb = x_ref[...].astype(BF16)

    def mm(lo, hi):
        return jnp.dot(xb, w_ref[:, lo:hi], preferred_element_type=F32)

    qkv_ref[...] = mm(0, DN_CH)
    o = DN_CH
    z_ref[...] = mm(o, o + DN_QK)
    o += DN_QK
    cos = cos_ref[...]
    sin = sin_ref[...]
    q = _rope_cols(mm(o, o + DA_W), cos, sin)
    q_ref[...] = (q * (DA_DQK ** -0.5)).astype(BF16)
    o += DA_W
    k = _rope_cols(mm(o, o + DA_W), cos, sin)
    kf_ref[...] = k
    kb_ref[...] = k.astype(BF16)
    o += DA_W
    v = mm(o, o + DA_W)
    vf_ref[...] = v
    vb_ref[...] = v.astype(BF16)
    o += DA_W
    ab_ref[...] = mm(o, o + LANES)


def _proj(x2d, w_r, cos_t, sin_t):
    m = x2d.shape[0]
    tm = _row_tile(m, 256)
    row = lambda w: pl.BlockSpec((tm, w), lambda i: (i, 0))
    outs = [(DN_CH, F32), (DN_QK, F32), (LANES, F32), (DA_W, BF16), (DA_W, F32), (DA_W, BF16), (DA_W, F32), (DA_W, BF16)]
    return pl.pallas_call(
        _proj_kernel,
        grid=(m // tm,),
        in_specs=[row(D_MODEL), pl.BlockSpec((D_MODEL, PROJ_W), lambda i: (0, 0)), row(LANES), row(LANES)],
        out_specs=[row(w) for w, _ in outs],
        out_shape=[jax.ShapeDtypeStruct((m, w), d) for w, d in outs],
        compiler_params=_cparams(("parallel",), 48 << 20),
        name="proj",
    )(x2d, w_r, cos_t, sin_t)


def _dot(a, b, prec=HI):
    return jnp.dot(a, b, precision=prec, preferred_element_type=F32)


def _dot_nt(a, b, prec=HI):
    return lax.dot_general(a, b, (((1,), (1,)), ((), ())), precision=prec, preferred_element_type=F32)


def _dot1(a, b):
    return jnp.dot(a.astype(BF16), b.astype(BF16), preferred_element_type=F32)


def _dot1_nt(a, b):
    return lax.dot_general(a.astype(BF16), b.astype(BF16), (((1,), (1,)), ((), ())), preferred_element_type=F32)


def _dot1_tn(a, b):
    return lax.dot_general(a.astype(BF16), b.astype(BF16), (((0,), (0,)), ((), ())), preferred_element_type=F32)


def _split_bf16(a):
    hi = a.astype(BF16)
    return hi, (a - hi.astype(F32)).astype(BF16)


def _dot3(a, b):
    ah, al = _split_bf16(a)
    bh, bl = _split_bf16(b)
    return (jnp.dot(ah, bh, preferred_element_type=F32)
            + (jnp.dot(ah, bl, preferred_element_type=F32) + jnp.dot(al, bh, preferred_element_type=F32)))


def _unit_lower_inverses(mats, c):
    r = lax.broadcasted_iota(I32, (c, c), 0)
    s = lax.broadcasted_iota(I32, (c, c), 1)
    base = min(c, 16)
    lb = base.bit_length() - 1
    ns = [jnp.where((r >> lb) == (s >> lb), -a, 0.0) for a in mats]
    eye = jnp.where(r == s, 1.0, 0.0)
    ps = [eye + n for n in ns]
    for _ in range(lb - 1):
        ns = [_dot3(n, n) for n in ns]
        ps = [p + _dot3(p, n) for p, n in zip(ps, ns)]
    while base < c:
        lb = base.bit_length() - 1
        e_mask = ((r >> (lb + 1)) == (s >> (lb + 1))) & (((r >> lb) & 1) == 1) & (((s >> lb) & 1) == 0)
        pes = [_dot3(p, jnp.where(e_mask, a, 0.0)) for p, a in zip(ps, mats)]
        ps = [p - _dot3(pe, p) for p, pe in zip(ps, pes)]
        base *= 2
    return ps


def _silu(x):
    return x * (1.0 / (1.0 + jnp.exp(-x)))


def _dn_kernel(c, nvalid, has_state, bs, *refs):
    if has_state:
        (qkv_ref, z_ref, ab_ref, cw_ref, hp_ref, nw_ref, s0_ref, cb0_ref,
         out_ref, sfin_ref, cfin_ref, s_scr, xbuf) = refs
    else:
        (qkv_ref, z_ref, ab_ref, cw_ref, hp_ref, nw_ref,
         out_ref, sfin_ref, cfin_ref, s_scr, xbuf) = refs
    i = pl.program_id(1)
    last = pl.num_programs(1) - 1
    base = SUBLANES
    hist = DN_CONV - 1

    @pl.when(i == 0)
    def _():
        if has_state:
            s_scr[...] = s0_ref[...]
            xbuf[:, base - hist:base, :] = cb0_ref[...]
        else:
            s_scr[...] = jnp.zeros_like(s_scr)
            xbuf[:, 0:base, :] = jnp.zeros((bs, base, DN_CH), F32)

    rowi = lax.broadcasted_iota(I32, (c, LANES), 0)
    live = rowi < nvalid
    rr = lax.broadcasted_iota(I32, (c, c), 0)
    cc = lax.broadcasted_iota(I32, (c, c), 1)
    incl = rr >= cc
    strict = rr > cc
    tril = jnp.where(incl, 1.0, 0.0)
    sel = jnp.where(lax.broadcasted_iota(I32, (SUBLANES, LANES), 0) == lax.broadcasted_iota(I32, (SUBLANES, LANES), 1),
                    1.0, 0.0)

    acts, gcums, growss, beta_alls = [], [], [], []
    for b in range(bs):
        xbuf[b, base:base + c, :] = qkv_ref[b]
        acc = xbuf[b, base - hist:base - hist + c, :] * cw_ref[0:1, :]
        for w in range(1, DN_CONV):
            acc = acc + xbuf[b, base - hist + w:base - hist + w + c, :] * cw_ref[w:w + 1, :]
        acts.append(_silu(acc))
        ab = ab_ref[b]
        sp_in = ab + hp_ref[1:2, :]
        softplus = jnp.maximum(sp_in, 0.0) + jnp.log(1.0 + jnp.exp(-jnp.abs(sp_in)))
        g_all = jnp.where(live, -jnp.exp(hp_ref[0:1, :]) * softplus, 0.0)
        beta_alls.append(jnp.where(live, 1.0 / (1.0 + jnp.exp(-ab)), 0.0))
        gcum = _dot(tril, g_all)
        gcums.append(gcum)
        growss.append(_dot_nt(sel, gcum))

    @pl.when(i == last)
    def _():
        cfin_ref[...] = xbuf[:, base + nvalid - hist:base + nvalid, :]

    xbuf[:, base - hist:base, :] = xbuf[:, base + c - hist:base + c, :]

    chains = [(b, h) for b in range(bs) for h in range(DN_HEADS)]
    n = range(len(chains))
    qs, ks, vs = [], [], []
    for b, h in chains:
        qh = acts[b][:, h * DN_D:(h + 1) * DN_D]
        kh = acts[b][:, DN_QK + h * DN_D:DN_QK + (h + 1) * DN_D]
        qs.append(qh * lax.rsqrt(jnp.sum(qh * qh, axis=-1, keepdims=True) + 1e-6) * (DN_D ** -0.5))
        ks.append(kh * lax.rsqrt(jnp.sum(kh * kh, axis=-1, keepdims=True) + 1e-6))
        vs.append(acts[b][:, 2 * DN_QK + h * DN_D:2 * DN_QK + (h + 1) * DN_D])
    betas = [beta_alls[b][:, DN_HEADS + h:DN_HEADS + h + 1] for b, h in chains]
    gcols = [gcums[b][:, h:h + 1] for b, h in chains]
    glasts = [gcums[b][c - 1:c, h:h + 1] for b, h in chains]
    decays = [jnp.exp(jnp.where(incl, gcols[x] - growss[b][h:h + 1, :], NEG))
              for x, (b, h) in enumerate(chains)]
    egs = [jnp.exp(g) for g in gcols]
    kbs = [ks[x] * betas[x] for x in n]
    amats = [jnp.where(strict, _dot1_nt(kbs[x], ks[x]) * decays[x], 0.0) for x in n]
    tinvs = _unit_lower_inverses(amats, c)
    us = [_dot3(tinvs[x], vs[x] * betas[x]) for x in n]
    ws = [_dot3(tinvs[x], kbs[x] * egs[x]) for x in n]
    aqks = [_dot1_nt(qs[x], ks[x]) * decays[x] for x in n]
    ss = [s_scr[b, h] for b, h in chains]
    v_news = [us[x] - _dot1(ws[x], ss[x]) for x in n]
    outs = [_dot1(qs[x] * egs[x], ss[x]) + _dot1(aqks[x], v_news[x]) for x in n]
    for x, (b, h) in enumerate(chains):
        s_scr[b, h] = ss[x] * jnp.exp(glasts[x]) + _dot1_tn(ks[x] * jnp.exp(glasts[x] - gcols[x]), v_news[x])
    for x, (b, h) in enumerate(chains):
        o = outs[x]
        zh = z_ref[b, :, h * DN_D:(h + 1) * DN_D]
        on = o * lax.rsqrt(jnp.mean(o * o, axis=-1, keepdims=True) + 1e-6) * nw_ref[...]
        out_ref[b, :, h * DN_D:(h + 1) * DN_D] = (on * _silu(zh)).astype(out_ref.dtype)

    @pl.when(i == last)
    def _():
        sfin_ref[...] = s_scr[...]


def _deltanet(qkv, z, ab, conv_w, hp, norm_w, c, nvalid, s0=None, cb0=None):
    nb, t_rows, _ = qkv.shape
    nc = t_rows // c
    bs = math.gcd(nb, DN_SEQS_PER_STEP)
    has_state = s0 is not None
    row = lambda w: pl.BlockSpec((bs, c, w), lambda b, i: (b, i, 0))
    full = lambda shape: pl.BlockSpec(shape, lambda b, i: (0,) * len(shape))
    state = pl.BlockSpec((bs, DN_HEADS, DN_D, DN_D), lambda b, i: (b, 0, 0, 0))
    cbuf = pl.BlockSpec((bs, DN_CONV - 1, DN_CH), lambda b, i: (b, 0, 0))
    in_specs = [row(DN_CH), row(DN_QK), row(LANES), full((DN_CONV, DN_CH)), full((SUBLANES, LANES)), full((1, DN_D))]
    args = [qkv, z, ab, conv_w, hp, norm_w]
    if has_state:
        in_specs += [state, cbuf]
        args += [s0, cb0]
    return pl.pallas_call(
        functools.partial(_dn_kernel, c, nvalid, has_state, bs),
        grid=(nb // bs, nc),
        in_specs=in_specs,
        out_specs=[row(DN_QK), state, cbuf],
        out_shape=[jax.ShapeDtypeStruct((nb, t_rows, DN_QK), BF16),
                   jax.ShapeDtypeStruct((nb, DN_HEADS, DN_D, DN_D), F32),
                   jax.ShapeDtypeStruct((nb, DN_CONV - 1, DN_CH), F32)],
        scratch_shapes=[pltpu.VMEM((bs, DN_HEADS, DN_D, DN_D), F32), pltpu.VMEM((bs, c + SUBLANES, DN_CH), F32)],
        compiler_params=_cparams(("parallel", "arbitrary"), 48 << 20),
        name="deltanet",
    )(*args)


def _da_finish(o1, o2, lam, nw, post):
    o = o1 - lam * o2
    return o * lax.rsqrt(jnp.mean(o * o, axis=-1, keepdims=True) + 1e-6) * nw * post


def _attn_kernel(tq, post, qi_ref, kj_ref, lam_ref, q_ref, k_ref, v_ref, nw_ref, out_ref, m_scr, acc_scr):
    i = qi_ref[pl.program_id(2)]
    j = kj_ref[pl.program_id(2)]
    nt = tq // LANES

    @pl.when(j == 0)
    def _():
        m_scr[...] = jnp.full_like(m_scr, NEG)
        acc_scr[...] = jnp.zeros_like(acc_scr)

    def update(masked):
        q = q_ref[...]
        k = k_ref[...]
        v1 = jnp.concatenate([v_ref[...], jnp.ones((tq, LANES), BF16)], axis=1)
        if masked:
            keep = lax.broadcasted_iota(I32, (tq, tq), 0) >= lax.broadcasted_iota(I32, (tq, tq), 1)
        halves = range(2)
        ss = [_dot_nt(q[:, c * DA_DQK:(c + 1) * DA_DQK], k[:, c * DA_DQK:(c + 1) * DA_DQK], prec=None)
              for c in halves]
        if masked:
            ss = [jnp.where(keep, s, NEG) for s in ss]
        m_olds = [m_scr[c] for c in halves]
        m_news = [jnp.maximum(m_olds[c], jnp.max(ss[c], axis=-1, keepdims=True)) for c in halves]
        ps = [jnp.concatenate([jnp.exp(ss[c][:, a * LANES:(a + 1) * LANES] - m_news[c]) for a in range(nt)], axis=1)
              for c in halves]
        alphas = [jnp.exp(m_olds[c] - m_news[c]) for c in halves]
        pvs = [jnp.dot(ps[c].astype(BF16), v1, preferred_element_type=F32) for c in halves]
        for c in halves:
            acc_scr[c] = jnp.concatenate([alphas[c], alphas[c]], axis=1) * acc_scr[c] + pvs[c]
            m_scr[c] = m_news[c]

    @pl.when(j < i)
    def _():
        update(False)

    @pl.when(j == i)
    def _():
        update(True)
        o1 = acc_scr[0, :, 0:LANES] / acc_scr[0, :, LANES:]
        o2 = acc_scr[1, :, 0:LANES] / acc_scr[1, :, LANES:]
        out_ref[...] = _da_finish(o1, o2, lam_ref[0:1, 0:1], nw_ref[...], post).astype(out_ref.dtype)


def _prompt_attention(q, k, v, lam_t, nw, nb, t, post):
    tq = _row_tile(t, 512)
    nq = t // tq
    pairs = [(i, j) for i in range(nq) for j in range(i + 1)]
    qi = jnp.asarray([p[0] for p in pairs], I32)
    kj = jnp.asarray([p[1] for p in pairs], I32)
    q_spec = pl.BlockSpec((tq, LANES), lambda b, h, s, qi, kj: (b * nq + qi[s], h))
    kv_spec = pl.BlockSpec((tq, LANES), lambda b, h, s, qi, kj: (b * nq + kj[s], h))
    const = pl.BlockSpec((1, LANES), lambda b, h, s, qi, kj: (0, 0))
    grid_spec = pltpu.PrefetchScalarGridSpec(
        num_scalar_prefetch=2,
        grid=(nb, DA_HEADS, len(pairs)),
        in_specs=[const, q_spec, kv_spec, kv_spec, const],
        out_specs=q_spec,
        scratch_shapes=[pltpu.VMEM((2, tq, LANES), F32), pltpu.VMEM((2, tq, 2 * LANES), F32)],
    )
    return pl.pallas_call(
        functools.partial(_attn_kernel, tq, post),
        grid_spec=grid_spec,
        out_shape=jax.ShapeDtypeStruct((nb * t, DA_W), BF16),
        compiler_params=_cparams(("parallel", "parallel", "arbitrary"), 32 << 20),
        name="prompt_attention",
    )(qi, kj, lam_t, q, k, v, nw)


def _paged_kernel(s_new, post, npg, pt_ref, lam_ref, qw_ref, *refs):
    k_refs, v_refs = refs[:npg], refs[npg:2 * npg]
    kn_ref, vn_ref, nw_ref, out_ref, m_scr, l_scr, acc_scr = refs[2 * npg:]
    p_id = pl.program_id(1)
    nrow = DA_HEADS * 2 * s_new
    qw = qw_ref[0]

    @pl.when(p_id == 0)
    def _():
        m_scr[...] = jnp.full_like(m_scr, NEG)
        l_scr[...] = jnp.zeros_like(l_scr)
        acc_scr[...] = jnp.zeros_like(acc_scr)

    def keep_mask(nk, causal):
        row = lax.broadcasted_iota(I32, (nrow, nk), 0)
        col = lax.broadcasted_iota(I32, (nrow, nk), 1)
        keep = (col & (DA_HEADS - 1)) == (row >> ((2 * s_new).bit_length() - 1))
        if causal:
            keep = keep & ((col >> (DA_HEADS.bit_length() - 1)) <= (row & (s_new - 1)))
        return keep

    def accumulate(kblks, vblks, keep):
        ss = [jnp.where(keep, _dot_nt(qw, kb.astype(BF16), prec=None), NEG) for kb in kblks]
        smax = ss[0]
        for s in ss[1:]:
            smax = jnp.maximum(smax, s)
        m_old = m_scr[...]
        m_new = jnp.maximum(m_old, jnp.max(smax, axis=-1, keepdims=True))
        ps = [jnp.where(keep, jnp.exp(s - m_new), 0.0) for s in ss]
        psum = ps[0]
        for p in ps[1:]:
            psum = psum + p
        alpha = jnp.exp(m_old - m_new)
        l_scr[...] = alpha * l_scr[...] + jnp.sum(psum, axis=-1, keepdims=True)
        pv = jnp.dot(ps[0].astype(BF16), vblks[0].astype(BF16), preferred_element_type=F32)
        for p, vb in zip(ps[1:], vblks[1:]):
            pv = pv + jnp.dot(p.astype(BF16), vb.astype(BF16), preferred_element_type=F32)
        acc_scr[...] = alpha * acc_scr[...] + pv
        m_scr[...] = m_new

    accumulate([r[0] for r in k_refs], [r[0] for r in v_refs], keep_mask(k_refs[0].shape[1], False))

    @pl.when(p_id == pl.num_programs(1) - 1)
    def _():
        accumulate([kn_ref[0]], [vn_ref[0]], keep_mask(kn_ref.shape[1], True))
        o = acc_scr[...] / l_scr[...]
        lam = lam_ref[0:1, 0:1]
        for h in range(DA_HEADS):
            o1 = o[h * 2 * s_new:h * 2 * s_new + s_new]
            o2 = o[h * 2 * s_new + s_new:(h + 1) * 2 * s_new]
            out_ref[0, h * s_new:(h + 1) * s_new, :] = _da_finish(o1, o2, lam, nw_ref[...], post)


def _paged_attention(page_table, lam_t, qw, cache_k, cache_v, k_new, v_new, nw, s_new, post):
    nb, n_pages = page_table.shape
    page_rows = cache_k.shape[1]
    nrow = DA_HEADS * 2 * s_new
    new_rows = k_new.shape[1]
    npg = math.gcd(n_pages, PAGES_PER_STEP)

    def page_spec(i):
        return pl.BlockSpec((1, page_rows, LANES), lambda b, p, pt: (pt[b, p * npg + i], 0, 0))

    grid_spec = pltpu.PrefetchScalarGridSpec(
        num_scalar_prefetch=1,
        grid=(nb, n_pages // npg),
        in_specs=[pl.BlockSpec((1, LANES), lambda b, p, pt: (0, 0)),
                  pl.BlockSpec((1, nrow, LANES), lambda b, p, pt: (b, 0, 0))]
                 + [page_spec(i) for i in range(npg)] * 2
                 + [pl.BlockSpec((1, new_rows, LANES), lambda b, p, pt: (b, 0, 0)),
                    pl.BlockSpec((1, new_rows, LANES), lambda b, p, pt: (b, 0, 0)),
                    pl.BlockSpec((1, LANES), lambda b, p, pt: (0, 0))],
        out_specs=pl.BlockSpec((1, DA_HEADS * s_new, LANES), lambda b, p, pt: (b, 0, 0)),
        scratch_shapes=[pltpu.VMEM((nrow, 1), F32), pltpu.VMEM((nrow, 1), F32), pltpu.VMEM((nrow, LANES), F32)],
    )
    return pl.pallas_call(
        functools.partial(_paged_kernel, s_new, post, npg),
        grid_spec=grid_spec,
        out_shape=jax.ShapeDtypeStruct((nb, DA_HEADS * s_new, LANES), F32),
        compiler_params=_cparams(("parallel", "arbitrary"), 32 << 20),
        name="paged_attention",
    )(page_table, lam_t, qw, *([cache_k] * npg), *([cache_v] * npg), k_new, v_new, nw)


def _layer_norm(y, g, b):
    mu = jnp.mean(y, axis=-1, keepdims=True)
    d = y - mu
    var = jnp.mean(d * d, axis=-1, keepdims=True)
    return d * lax.rsqrt(var + 1e-5) * g + b


def _mix_kernel(alpha, x_ref, dn_ref, da_ref, w_ref, g_ref, b_ref, x1_ref, x1t_ref):
    mix = (jnp.dot(dn_ref[...], w_ref[0:DN_QK, :], preferred_element_type=F32)
           + jnp.dot(da_ref[...], w_ref[DN_QK:, :], preferred_element_type=F32))
    x1 = _layer_norm(alpha * x_ref[...] + mix, g_ref[...], b_ref[...])
    x1_ref[...] = x1
    for s in range(SUBLANES):
        x1t_ref[:, s, :] = x1[:, s * LANES:(s + 1) * LANES]


def _mix(x2d, dn, da, w_out, g, b, alpha):
    m = x2d.shape[0]
    tm = _row_tile(m, 256)
    row = lambda w: pl.BlockSpec((tm, w), lambda i: (i, 0))
    full = lambda shape: pl.BlockSpec(shape, lambda i: (0,) * len(shape))
    return pl.pallas_call(
        functools.partial(_mix_kernel, alpha),
        grid=(m // tm,),
        in_specs=[row(D_MODEL), row(DN_QK), row(DA_W), full((D_MODEL, D_MODEL)), full((1, D_MODEL)), full((1, D_MODEL))],
        out_specs=[row(D_MODEL), pl.BlockSpec((tm, SUBLANES, LANES), lambda i: (i, 0, 0))],
        out_shape=[jax.ShapeDtypeStruct((m, D_MODEL), F32), jax.ShapeDtypeStruct((m, SUBLANES, LANES), F32)],
        compiler_params=_cparams(("parallel",), 32 << 20),
        name="mix_ln1",
    )(x2d, dn, da, w_out, g, b)


def _top16_rows(vals, payload=None):
    r = vals.shape[0]
    iota = lax.broadcasted_iota(I32, vals.shape, 0)
    tops, idxs = [], []
    for _ in range(PEER_TOPK):
        m = jnp.max(vals, axis=0, keepdims=True)
        am = jnp.min(jnp.where(vals == m, iota, r), axis=0, keepdims=True)
        hit = iota == am
        tops.append(m)
        idxs.append(am if payload is None else jnp.sum(jnp.where(hit, payload, 0), axis=0, keepdims=True))
        vals = jnp.where(hit, -jnp.inf, vals)
    return tops, idxs


def _route_kernel(x_ref, wq_ref, sk_ref, code_ref, gate_ref):
    xb = x_ref[...].astype(BF16)
    half = PEER_DK // 2
    gates, codes = [], []
    for h in range(PEER_HEADS):
        picks = []
        for c in range(2):
            col = (h * 2 + c) * half
            q = jnp.dot(xb, wq_ref[:, col:col + half], preferred_element_type=F32)
            st = _dot_nt(sk_ref[c], q.astype(BF16), prec=None)
            picks.append(_top16_rows(st))
        (s1, i1), (s2, i2) = picks
        s2m = jnp.concatenate(s2, axis=0)
        i2m = jnp.concatenate(i2, axis=0)
        sub8 = lax.broadcasted_iota(I32, (SUBLANES, s2m.shape[1]), 0)
        cand = [s1[0] + s2m]
        cidx = [i1[0] * N_KEYS + i2m]
        for a in range(1, SUBLANES):
            cand.append(jnp.where(sub8 < PEER_TOPK // (a + 1), s1[a] + s2m[:SUBLANES], -jnp.inf))
            cidx.append(i1[a] * N_KEYS + i2m[:SUBLANES])
        cand.append(jnp.concatenate(s1[SUBLANES:], axis=0) + s2[0])
        cidx.append(jnp.concatenate(i1[SUBLANES:], axis=0) * N_KEYS + i2[0])
        top, eidx = _top16_rows(jnp.concatenate(cand, axis=0), jnp.concatenate(cidx, axis=0))
        ex = [jnp.exp(t - top[0]) for t in top]
        den = ex[0]
        for e in ex[1:]:
            den = den + e
        gates.append(jnp.concatenate(ex, axis=0) / den)
        e = jnp.concatenate(eidx, axis=0)
        codes.append(((e & (HALF_EXPERTS - 1)) << (U_ROW_SHIFT + 3)) | (16 - ((e >> 13) << 4)))
    gate_ref[...] = jnp.concatenate(gates, axis=0).T
    code_ref[...] = jnp.concatenate(codes, axis=0).T


def _route(x1, wq, sk):
    m = x1.shape[0]
    tm = _row_tile(m, 256)
    full = lambda shape: pl.BlockSpec(shape, lambda i: (0,) * len(shape))
    row = pl.BlockSpec((tm, N_PAIRS), lambda i: (i, 0))
    return pl.pallas_call(
        _route_kernel,
        grid=(m // tm,),
        in_specs=[pl.BlockSpec((tm, D_MODEL), lambda i: (i, 0)), full(wq.shape), full(sk.shape)],
        out_specs=[row, row],
        out_shape=[jax.ShapeDtypeStruct((m, N_PAIRS), I32), jax.ShapeDtypeStruct((m, N_PAIRS), F32)],
        compiler_params=_cparams(("parallel",), 32 << 20),
        name="peer_route",
    )(x1, wq, sk)


def _expert_row(tab_ref, row, shift_v):
    word = tab_ref[pl.ds(pl.multiple_of(row, SUBLANES), SUBLANES), :]
    return pltpu.bitcast((word << shift_v) & HI_HALF, F32)


def _splat(s):
    return jnp.full((SUBLANES, LANES), s, I32)


def _fold_pair(a, b, s, sub):
    m = (sub & s) == 0
    return jnp.where(m, a, b) + pltpu.roll(jnp.where(m, b, a), s, axis=0)


def _sublane_sums(p, sub):
    c = [_fold_pair(p[2 * i], p[2 * i + 1], 1, sub) for i in range(4)]
    d = [_fold_pair(c[0], c[1], 2, sub), _fold_pair(c[2], c[3], 2, sub)]
    return _fold_pair(d[0], d[1], 4, sub)


def _peer_u_kernel(tb, rt, code_ref, x_ref, gate_ref, codev_ref, tab_ref, wc_ref, part_scr, h_scr):
    sub = lax.broadcasted_iota(I32, (SUBLANES, LANES), 0)
    ones = jnp.ones((SUBLANES, LANES), BF16)

    def batch(o, carry):
        def token(tt, carry):
            t = o * rt + tt
            xv = x_ref[t]
            for g in range(N_PAIRS // SUBLANES):
                codes = [code_ref[t * N_PAIRS + g * SUBLANES + i] for i in range(SUBLANES)]
                prods = [_expert_row(tab_ref, lax.shift_right_logical(c, U_ROW_SHIFT), _splat(c) & 31) * xv
                         for c in codes]
                row0 = pl.multiple_of(tt * N_PAIRS + g * SUBLANES, SUBLANES)
                part_scr[pl.ds(row0, SUBLANES), :] = _sublane_sums(prods, sub)
            return carry

        lax.fori_loop(0, rt, token, 0)
        hi, lo = _split_bf16(part_scr[...])
        tot = (lax.dot_general(ones, hi, (((1,), (1,)), ((), ())), preferred_element_type=F32)
               + lax.dot_general(ones, lo, (((1,), (1,)), ((), ())), preferred_element_type=F32))
        for q in range(rt // SUBLANES):
            tile = tot[:, q * SUBLANES * N_PAIRS:(q * SUBLANES + 1) * N_PAIRS]
            for j in range(1, SUBLANES):
                tile = jnp.where(sub == j, tot[:, (q * SUBLANES + j) * N_PAIRS:(q * SUBLANES + j + 1) * N_PAIRS], tile)
            h_scr[pl.ds(pl.multiple_of(o * rt + q * SUBLANES, SUBLANES), SUBLANES), :] = tile
        return carry

    lax.fori_loop(0, tb // rt, batch, 0)
    h = h_scr[...]
    w = gate_ref[...] * (0.5 * h * (1.0 + lax.erf(h * (0.5 ** 0.5))))
    cv = codev_ref[...]
    code_v = lax.shift_right_logical(cv, U_ROW_SHIFT) | ((cv >> 4) & 1)
    wc_ref[...] = pltpu.bitcast(w.astype(BF16).astype(F32), I32) | code_v


def _peer_v_kernel(tb, wc_ref, tab_ref, out_ref):
    nacc = 4

    def token(t, carry):
        accs = [jnp.zeros((SUBLANES, LANES), F32) for _ in range(nacc)]
        for k in range(N_PAIRS):
            wc = wc_ref[t * N_PAIRS + k]
            wc_v = _splat(wc)
            weight = pltpu.bitcast(wc_v & HI_HALF, F32)
            row = _expert_row(tab_ref, wc & ROW_MASK, (wc_v & 1) << 4)
            accs[k % nacc] = accs[k % nacc] + row * weight
        out_ref[t] = (accs[0] + accs[1]) + (accs[2] + accs[3])
        return carry

    lax.fori_loop(0, tb, token, 0)


def _peer_tb(m):
    return LANES if m % LANES == 0 else m


def _peer_u(code, x1t, gate, tab):
    m = x1t.shape[0]
    tb = _peer_tb(m)
    rt = math.gcd(tb, LANE_SUM_TOKENS)
    row = pl.BlockSpec((tb, N_PAIRS), lambda i: (i, 0))
    return pl.pallas_call(
        functools.partial(_peer_u_kernel, tb, rt),
        grid=(m // tb,),
        in_specs=[pl.BlockSpec((tb * N_PAIRS,), lambda i: (i,), memory_space=pltpu.SMEM),
                  pl.BlockSpec((tb, SUBLANES, LANES), lambda i: (i, 0, 0)), row, row,
                  pl.BlockSpec(tab.shape, lambda i: (0, 0))],
        out_specs=row,
        out_shape=jax.ShapeDtypeStruct((m, N_PAIRS), I32),
        scratch_shapes=[pltpu.VMEM((rt * N_PAIRS, LANES), F32), pltpu.VMEM((tb, N_PAIRS), F32)],
        compiler_params=_cparams(("arbitrary",), VMEM_PHYSICAL - (8 << 20)),
        name="peer_u",
    )(code.reshape(-1), x1t, gate, code, tab)


def _peer_v(wc, tab):
    m = wc.shape[0]
    tb = _peer_tb(m)
    return pl.pallas_call(
        functools.partial(_peer_v_kernel, tb),
        grid=(m // tb,),
        in_specs=[pl.BlockSpec((tb * N_PAIRS,), lambda i: (i,), memory_space=pltpu.SMEM),
                  pl.BlockSpec(tab.shape, lambda i: (0, 0))],
        out_specs=pl.BlockSpec((tb, SUBLANES, LANES), lambda i: (i, 0, 0)),
        out_shape=jax.ShapeDtypeStruct((m, SUBLANES, LANES), F32),
        compiler_params=_cparams(("arbitrary",), VMEM_PHYSICAL - (8 << 20)),
        name="peer_v",
    )(wc.reshape(-1), tab)


def _ln2_kernel(alpha, x1_ref, ffn_ref, g_ref, b_ref, y_ref):
    ffn = jnp.concatenate([ffn_ref[:, s, :] for s in range(SUBLANES)], axis=1)
    y_ref[...] = _layer_norm(alpha * x1_ref[...] + ffn, g_ref[...], b_ref[...])


def _ln2(x1, ffn_t, g, b, alpha):
    m = x1.shape[0]
    tm = _row_tile(m, 256)
    row = pl.BlockSpec((tm, D_MODEL), lambda i: (i, 0))
    full = pl.BlockSpec((1, D_MODEL), lambda i: (0, 0))
    return pl.pallas_call(
        functools.partial(_ln2_kernel, alpha),
        grid=(m // tm,),
        in_specs=[row, pl.BlockSpec((tm, SUBLANES, LANES), lambda i: (i, 0, 0)), full, full],
        out_specs=row,
        out_shape=jax.ShapeDtypeStruct((m, D_MODEL), F32),
        compiler_params=_cparams(("parallel",), 32 << 20),
        name="ln2",
    )(x1, ffn_t, g, b)


def _pack_table(tab):
    bits = lax.bitcast_convert_type(tab.astype(BF16), jnp.uint16).astype(jnp.uint32)
    word = bits[:HALF_EXPERTS] | (bits[HALF_EXPERTS:] << 16)
    return lax.bitcast_convert_type(word, I32).reshape(HALF_EXPERTS * SUBLANES, LANES)


def _rope_tables(pos):
    half = DA_DQK // 2
    inv_freq = ROPE_THETA ** (-jnp.arange(half, dtype=F32) * 2.0 / DA_DQK)
    ang = pos.astype(F32)[:, None] * inv_freq[None, :]
    cos, sin = jnp.cos(ang), jnp.sin(ang)
    return jnp.tile(cos, (1, LANES // half)), jnp.tile(jnp.concatenate([-sin, sin], axis=1), (1, LANES // DA_DQK))


def _peer_and_norm(x1, x1t, p):
    code, gate = _route(x1, p["wq"], p["sk"])
    wc = _peer_u(code, x1t, gate, p["u_tab"])
    ffn_t = _peer_v(wc, p["v_tab"])
    return _ln2(x1, ffn_t, p["ln2_g"], p["ln2_b"], p["alpha"])


def kernel(x_prompt, x_sample, cache_k, cache_v, state_dn, state_conv, page_table, w_in, dn_conv_w, dn_a_log,
           dn_dt_bias, dn_norm_w, da_lambda_q1, da_lambda_k1, da_lambda_q2, da_lambda_k2, da_norm_w, w_out,
           ln1_g, ln1_b, peer_wq, peer_sub_keys, peer_u, peer_v, ln2_g, ln2_b):
    depth = w_in.shape[0]
    assert depth == 1, "single-layer trunk"
    nb, t, _ = x_prompt.shape
    nsb, s_new, _ = x_sample.shape
    assert s_new & (s_new - 1) == 0, "sample length must be a power of two"
    n_pages, page = page_table.shape[1], cache_k.shape[2]
    past_len = n_pages * page
    alpha = (2 * depth) ** 0.25
    lam_init = 0.8 - 0.6 * math.exp(-0.3 * 0)
    post = 1.0 - lam_init
    lam = (jnp.exp(jnp.sum(da_lambda_q1[0] * da_lambda_k1[0])) - jnp.exp(jnp.sum(da_lambda_q2[0] * da_lambda_k2[0]))
           + lam_init)
    lam_t = jnp.full((1, LANES), lam, F32)

    o_ab = DN_CH + DN_QK
    w0 = w_in[0]
    w_r = jnp.concatenate([w0[:, :o_ab], w0[:, o_ab + 2 * DN_HEADS:], w0[:, o_ab:o_ab + 2 * DN_HEADS],
                           jnp.zeros((D_MODEL, LANES - 2 * DN_HEADS), F32)], axis=1).astype(BF16)
    hp = jnp.zeros((SUBLANES, LANES), F32).at[0, :DN_HEADS].set(dn_a_log[0]).at[1, :DN_HEADS].set(dn_dt_bias[0])
    dn_nw = dn_norm_w[0].reshape(1, DN_D)
    da_nw = da_norm_w[0].reshape(1, DA_DV)
    p = dict(wq=peer_wq[0].astype(BF16), sk=peer_sub_keys[0].astype(BF16), u_tab=_pack_table(peer_u[0]),
             v_tab=_pack_table(peer_v[0]), ln2_g=ln2_g[0].reshape(1, D_MODEL), ln2_b=ln2_b[0].reshape(1, D_MODEL),
             alpha=alpha)
    w_out_b = w_out[0].astype(BF16)
    g1, b1 = ln1_g[0].reshape(1, D_MODEL), ln1_b[0].reshape(1, D_MODEL)

    xp = x_prompt.reshape(nb * t, D_MODEL)
    cos_p, sin_p = _rope_tables(jnp.tile(jnp.arange(t), nb))
    qkv, z, ab, q, kf, kb, vf, vb = _proj(xp, w_r, cos_p, sin_p)
    c = min(DN_CHUNK, t)
    seq = lambda a: a.reshape(nb, t, a.shape[-1])
    dn, s_p, c_p = _deltanet(seq(qkv), seq(z), seq(ab), dn_conv_w[0], hp, dn_nw, c, c)
    da = _prompt_attention(q, kb, vb, lam_t, da_nw, nb, t, post)
    x1, x1t = _mix(xp, dn.reshape(nb * t, DN_QK), da, w_out_b, g1, b1, alpha)
    y_p = _peer_and_norm(x1, x1t, p).reshape(nb, t, D_MODEL)
    k_p = kf.reshape(1, nb, t, DA_HEADS, 2 * DA_DQK)
    v_p = vf.reshape(1, nb, t, DA_HEADS, DA_DV)

    xs = x_sample.reshape(nsb * s_new, D_MODEL)
    cos_s, sin_s = _rope_tables(jnp.tile(past_len + jnp.arange(s_new), nsb))
    qkv, z, ab, q, kf, kb, vf, vb = _proj(xs, w_r, cos_s, sin_s)
    cs = SUBLANES * (-(-s_new // SUBLANES))
    pad = lambda a: jnp.pad(a.reshape(nsb, s_new, -1), ((0, 0), (0, cs - s_new), (0, 0)))
    dn, s_s, c_s = _deltanet(pad(qkv), pad(z), pad(ab), dn_conv_w[0], hp, dn_nw, cs, s_new,
                             s0=state_dn[0], cb0=state_conv[0])
    dn = dn[:, :s_new].reshape(nsb * s_new, DN_QK)
    q5 = q.reshape(nsb, s_new, DA_HEADS, 2, DA_DQK).transpose(0, 2, 3, 1, 4)
    qw = (q5[:, :, :, :, None, :] * jnp.eye(2, dtype=BF16)[None, None, :, None, :, None]).reshape(
        nsb, DA_HEADS * 2 * s_new, 2 * DA_DQK)
    new_rows = s_new * DA_HEADS
    new_pad = SUBLANES * 2 * (-(-new_rows // (SUBLANES * 2)))
    padn = lambda a: jnp.pad(a.reshape(nsb, new_rows, LANES), ((0, 0), (0, new_pad - new_rows), (0, 0)))
    da = _paged_attention(page_table, lam_t, qw, cache_k[0].reshape(-1, page * DA_HEADS, LANES),
                          cache_v[0].reshape(-1, page * DA_HEADS, LANES), padn(kf), padn(vf), da_nw, s_new, post)
    da = da.reshape(nsb, DA_HEADS, s_new, DA_DV).transpose(0, 2, 1, 3).reshape(nsb * s_new, DA_W).astype(BF16)
    x1, x1t = _mix(xs, dn, da, w_out_b, g1, b1, alpha)
    y_s = _peer_and_norm(x1, x1t, p).reshape(nsb, s_new, D_MODEL)
    k_s = kf.reshape(1, nsb, s_new, DA_HEADS, 2 * DA_DQK)
    v_s = vf.reshape(1, nsb, s_new, DA_HEADS, DA_DV)

    return (y_p, y_s, k_p, v_p, k_s, v_s, s_p[None], c_p[None], s_s[None], c_s[None])
```

```python
import functools
import math

import jax
import jax.numpy as jnp
from jax import lax
from jax.experimental import pallas as pl
from jax.experimental.pallas import tpu as pltpu

F32 = jnp.float32
BF16 = jnp.bfloat16
I32 = jnp.int32
HI = lax.Precision.HIGHEST

LANES = 128
SUBLANES = 8
VMEM_PHYSICAL = 64 * 1024 * 1024

D_MODEL = 1024
DN_HEADS = 4
DN_D = 128
DN_CONV = 4
DN_QK = DN_HEADS * DN_D
DN_CH = 3 * DN_QK
DN_CHUNK = 64
DA_HEADS = 4
DA_DQK = 64
DA_DV = 128
DA_W = DA_HEADS * DA_DV
ROPE_THETA = 10000.0
PEER_HEADS = 8
PEER_DK = 256
N_KEYS = 128
PEER_TOPK = 16
N_PAIRS = PEER_HEADS * PEER_TOPK
HALF_EXPERTS = N_KEYS * N_KEYS // 2
PROJ_W = DN_CH + DN_QK + 3 * DA_W + LANES
NEG = -1e30
PAGES_PER_STEP = 16
DN_SEQS_PER_STEP = 4
LANE_SUM_TOKENS = 32
ATTN_HEADS = 2
U_ROW_SHIFT = 5
ROW_MASK = 0xFFF8
HI_HALF = -65536


def _cparams(sem, vmem_bytes):
    return pltpu.CompilerParams(dimension_semantics=sem, vmem_limit_bytes=int(vmem_bytes))


def _row_tile(m, cap):
    t = cap
    while m % t:
        t //= 2
    assert t >= SUBLANES, (m, cap)
    return t


def _rope_cols(t, cos, sin_signed):
    lane = lax.broadcasted_iota(I32, (t.shape[0], LANES), 1)
    first = (lane & (DA_DQK - 1)) < (DA_DQK // 2)
    outs = []
    for h in range(DA_HEADS):
        th = t[:, h * LANES:(h + 1) * LANES]
        partner = jnp.where(first, pltpu.roll(th, LANES - DA_DQK // 2, axis=1), pltpu.roll(th, DA_DQK // 2, axis=1))
        outs.append(th * cos + partner * sin_signed)
    return jnp.concatenate(outs, axis=1)


def _proj_kernel(x_ref, w_ref, cos_ref, sin_ref, qkv_ref, z_ref, ab_ref, q_ref, kf_ref, kb_ref, vf_ref, vb_ref):
    xb = x_ref[...].astype(BF16)

    def mm(lo, hi):
        return jnp.dot(xb, w_ref[:, lo:hi], preferred_element_type=F32)

    qkv_ref[...] = mm(0, DN_CH)
    o = DN_CH
    z_ref[...] = mm(o, o + DN_QK)
    o += DN_QK
    cos = cos_ref[...]
    sin = sin_ref[...]
    q = _rope_cols(mm(o, o + DA_W), cos, sin)
    q_ref[...] = (q * (DA_DQK ** -0.5)).astype(BF16)
    o += DA_W
    k = _rope_cols(mm(o, o + DA_W), cos, sin)
    kf_ref[...] = k
    kb_ref[...] = k.astype(BF16)
    o += DA_W
    v = mm(o, o + DA_W)
    vf_ref[...] = v
    vb_ref[...] = v.astype(BF16)
    o += DA_W
    ab_ref[...] = mm(o, o + LANES)


def _proj(x2d, w_r, cos_t, sin_t):
    m = x2d.shape[0]
    tm = _row_tile(m, 256)
    row = lambda w: pl.BlockSpec((tm, w), lambda i: (i, 0))
    outs = [(DN_CH, F32), (DN_QK, F32), (LANES, F32), (DA_W, BF16), (DA_W, F32), (DA_W, BF16), (DA_W, F32), (DA_W, BF16)]
    return pl.pallas_call(
        _proj_kernel,
        grid=(m // tm,),
        in_specs=[row(D_MODEL), pl.BlockSpec((D_MODEL, PROJ_W), lambda i: (0, 0)), row(LANES), row(LANES)],
        out_specs=[row(w) for w, _ in outs],
        out_shape=[jax.ShapeDtypeStruct((m, w), d) for w, d in outs],
        compiler_params=_cparams(("parallel",), 48 << 20),
        name="proj",
    )(x2d, w_r, cos_t, sin_t)


def _dot(a, b, prec=HI):
    return jnp.dot(a, b, precision=prec, preferred_element_type=F32)


def _dot_nt(a, b, prec=HI):
    return lax.dot_general(a, b, (((1,), (1,)), ((), ())), precision=prec, preferred_element_type=F32)


def _dot1(a, b):
    return jnp.dot(a.astype(BF16), b.astype(BF16), preferred_element_type=F32)


def _dot1_nt(a, b):
    return lax.dot_general(a.astype(BF16), b.astype(BF16), (((1,), (1,)), ((), ())), preferred_element_type=F32)


def _dot1_tn(a, b):
    return lax.dot_general(a.astype(BF16), b.astype(BF16), (((0,), (0,)), ((), ())), preferred_element_type=F32)


def _split_bf16(a):
    hi = a.astype(BF16)
    return hi, (a - hi.astype(F32)).astype(BF16)


def _dot3(a, b):
    ah, al = _split_bf16(a)
    bh, bl = _split_bf16(b)
    return (jnp.dot(ah, bh, preferred_element_type=F32)
            + (jnp.dot(ah, bl, preferred_element_type=F32) + jnp.dot(al, bh, preferred_element_type=F32)))


def _unit_lower_inverses(mats, c):
    r = lax.broadcasted_iota(I32, (c, c), 0)
    s = lax.broadcasted_iota(I32, (c, c), 1)
    base = min(c, 16)
    lb = base.bit_length() - 1
    ns = [jnp.where((r >> lb) == (s >> lb), -a, 0.0) for a in mats]
    eye = jnp.where(r == s, 1.0, 0.0)
    ps = [eye + n for n in ns]
    for _ in range(lb - 1):
        ns = [_dot3(n, n) for n in ns]
        ps = [p + _dot3(p, n) for p, n in zip(ps, ns)]
    while base < c:
        lb = base.bit_length() - 1
        e_mask = ((r >> (lb + 1)) == (s >> (lb + 1))) & (((r >> lb) & 1) == 1) & (((s >> lb) & 1) == 0)
        pes = [_dot3(p, jnp.where(e_mask, a, 0.0)) for p, a in zip(ps, mats)]
        ps = [p - _dot3(pe, p) for p, pe in zip(ps, pes)]
        base *= 2
    return ps


def _silu(x):
    return x * (1.0 / (1.0 + jnp.exp(-x)))


def _dn_kernel(c, nvalid, has_state, bs, *refs):
    if has_state:
        (qkv_ref, z_ref, ab_ref, cw_ref, hp_ref, nw_ref, s0_ref, cb0_ref,
         out_ref, sfin_ref, cfin_ref, s_scr, xbuf) = refs
    else:
        (qkv_ref, z_ref, ab_ref, cw_ref, hp_ref, nw_ref,
         out_ref, sfin_ref, cfin_ref, s_scr, xbuf) = refs
    i = pl.program_id(1)
    last = pl.num_programs(1) - 1
    base = SUBLANES
    hist = DN_CONV - 1

    @pl.when(i == 0)
    def _():
        if has_state:
            s_scr[...] = s0_ref[...]
            xbuf[:, base - hist:base, :] = cb0_ref[...]
        else:
            s_scr[...] = jnp.zeros_like(s_scr)
            xbuf[:, 0:base, :] = jnp.zeros((bs, base, DN_CH), F32)

    rowi = lax.broadcasted_iota(I32, (c, LANES), 0)
    live = rowi < nvalid
    rr = lax.broadcasted_iota(I32, (c, c), 0)
    cc = lax.broadcasted_iota(I32, (c, c), 1)
    incl = rr >= cc
    strict = rr > cc
    tril = jnp.where(incl, 1.0, 0.0)
    sel = jnp.where(lax.broadcasted_iota(I32, (SUBLANES, LANES), 0) == lax.broadcasted_iota(I32, (SUBLANES, LANES), 1),
                    1.0, 0.0)

    acts, gcums, growss, beta_alls = [], [], [], []
    for b in range(bs):
        xbuf[b, base:base + c, :] = qkv_ref[b]
        acc = xbuf[b, base - hist:base - hist + c, :] * cw_ref[0:1, :]
        for w in range(1, DN_CONV):
            acc = acc + xbuf[b, base - hist + w:base - hist + w + c, :] * cw_ref[w:w + 1, :]
        acts.append(_silu(acc))
        ab = ab_ref[b]
        sp_in = ab + hp_ref[1:2, :]
        softplus = jnp.maximum(sp_in, 0.0) + jnp.log(1.0 + jnp.exp(-jnp.abs(sp_in)))
        g_all = jnp.where(live, -jnp.exp(hp_ref[0:1, :]) * softplus, 0.0)
        beta_alls.append(jnp.where(live, 1.0 / (1.0 + jnp.exp(-ab)), 0.0))
        gcum = _dot(tril, g_all)
        gcums.append(gcum)
        growss.append(_dot_nt(sel, gcum))

    @pl.when(i == last)
    def _():
        cfin_ref[...] = xbuf[:, base + nvalid - hist:base + nvalid, :]

    xbuf[:, base - hist:base, :] = xbuf[:, base + c - hist:base + c, :]

    chains = [(b, h) for b in range(bs) for h in range(DN_HEADS)]
    n = range(len(chains))
    qs, ks, vs = [], [], []
    for b, h in chains:
        qh = acts[b][:, h * DN_D:(h + 1) * DN_D]
        kh = acts[b][:, DN_QK + h * DN_D:DN_QK + (h + 1) * DN_D]
        qs.append(qh * lax.rsqrt(jnp.sum(qh * qh, axis=-1, keepdims=True) + 1e-6) * (DN_D ** -0.5))
        ks.append(kh * lax.rsqrt(jnp.sum(kh * kh, axis=-1, keepdims=True) + 1e-6))
        vs.append(acts[b][:, 2 * DN_QK + h * DN_D:2 * DN_QK + (h + 1) * DN_D])
    betas = [beta_alls[b][:, DN_HEADS + h:DN_HEADS + h + 1] for b, h in chains]
    gcols = [gcums[b][:, h:h + 1] for b, h in chains]
    glasts = [gcums[b][c - 1:c, h:h + 1] for b, h in chains]
    decays = [jnp.exp(jnp.where(incl, gcols[x] - growss[b][h:h + 1, :], NEG))
              for x, (b, h) in enumerate(chains)]
    egs = [jnp.exp(g) for g in gcols]
    kbs = [ks[x] * betas[x] for x in n]
    amats = [jnp.where(strict, _dot1_nt(kbs[x], ks[x]) * decays[x], 0.0) for x in n]
    tinvs = _unit_lower_inverses(amats, c)
    us = [_dot3(tinvs[x], vs[x] * betas[x]) for x in n]
    ws = [_dot3(tinvs[x], kbs[x] * egs[x]) for x in n]
    aqks = [_dot1_nt(qs[x], ks[x]) * decays[x] for x in n]
    ss = [s_scr[b, h] for b, h in chains]
    v_news = [us[x] - _dot1(ws[x], ss[x]) for x in n]
    outs = [_dot1(qs[x] * egs[x], ss[x]) + _dot1(aqks[x], v_news[x]) for x in n]
    for x, (b, h) in enumerate(chains):
        s_scr[b, h] = ss[x] * jnp.exp(glasts[x]) + _dot1_tn(ks[x] * jnp.exp(glasts[x] - gcols[x]), v_news[x])
    for x, (b, h) in enumerate(chains):
        o = outs[x]
        zh = z_ref[b, :, h * DN_D:(h + 1) * DN_D]
        on = o * lax.rsqrt(jnp.mean(o * o, axis=-1, keepdims=True) + 1e-6) * nw_ref[...]
        out_ref[b, :, h * DN_D:(h + 1) * DN_D] = (on * _silu(zh)).astype(out_ref.dtype)

    @pl.when(i == last)
    def _():
        sfin_ref[...] = s_scr[...]


def _deltanet(qkv, z, ab, conv_w, hp, norm_w, c, nvalid, s0=None, cb0=None):
    nb, t_rows, _ = qkv.shape
    nc = t_rows // c
    bs = math.gcd(nb, DN_SEQS_PER_STEP)
    has_state = s0 is not None
    row = lambda w: pl.BlockSpec((bs, c, w), lambda b, i: (b, i, 0))
    full = lambda shape: pl.BlockSpec(shape, lambda b, i: (0,) * len(shape))
    state = pl.BlockSpec((bs, DN_HEADS, DN_D, DN_D), lambda b, i: (b, 0, 0, 0))
    cbuf = pl.BlockSpec((bs, DN_CONV - 1, DN_CH), lambda b, i: (b, 0, 0))
    in_specs = [row(DN_CH), row(DN_QK), row(LANES), full((DN_CONV, DN_CH)), full((SUBLANES, LANES)), full((1, DN_D))]
    args = [qkv, z, ab, conv_w, hp, norm_w]
    if has_state:
        in_specs += [state, cbuf]
        args += [s0, cb0]
    return pl.pallas_call(
        functools.partial(_dn_kernel, c, nvalid, has_state, bs),
        grid=(nb // bs, nc),
        in_specs=in_specs,
        out_specs=[row(DN_QK), state, cbuf],
        out_shape=[jax.ShapeDtypeStruct((nb, t_rows, DN_QK), BF16),
                   jax.ShapeDtypeStruct((nb, DN_HEADS, DN_D, DN_D), F32),
                   jax.ShapeDtypeStruct((nb, DN_CONV - 1, DN_CH), F32)],
        scratch_shapes=[pltpu.VMEM((bs, DN_HEADS, DN_D, DN_D), F32), pltpu.VMEM((bs, c + SUBLANES, DN_CH), F32)],
        compiler_params=_cparams(("parallel", "arbitrary"), 48 << 20),
        name="deltanet",
    )(*args)


def _da_finish(o1, o2, lam, nw, post):
    o = o1 - lam * o2
    return o * lax.rsqrt(jnp.mean(o * o, axis=-1, keepdims=True) + 1e-6) * nw * post


def _attn_kernel(tq, post, qi_ref, kj_ref, lam_ref, q_ref, k_ref, v_ref, nw_ref, out_ref, m_scr, acc_scr):
    i = qi_ref[pl.program_id(2)]
    j = kj_ref[pl.program_id(2)]
    nt = tq // LANES

    @pl.when(j == 0)
    def _():
        m_scr[...] = jnp.full_like(m_scr, NEG)
        acc_scr[...] = jnp.zeros_like(acc_scr)

    def update(masked):
        q = q_ref[...]
        k = k_ref[...]
        ones = jnp.ones((tq, LANES), BF16)
        v1s = [jnp.concatenate([v_ref[:, hh * LANES:(hh + 1) * LANES], ones], axis=1) for hh in range(ATTN_HEADS)]
        if masked:
            keep = lax.broadcasted_iota(I32, (tq, tq), 0) >= lax.broadcasted_iota(I32, (tq, tq), 1)
        chains = range(2 * ATTN_HEADS)
        ss = [_dot_nt(q[:, x * DA_DQK:(x + 1) * DA_DQK], k[:, x * DA_DQK:(x + 1) * DA_DQK], prec=None)
              for x in chains]
        if masked:
            ss = [jnp.where(keep, s, NEG) for s in ss]
        m_olds = [m_scr[x] for x in chains]
        m_news = [jnp.maximum(m_olds[x], jnp.max(ss[x], axis=-1, keepdims=True)) for x in chains]
        ps = [jnp.concatenate([jnp.exp(ss[x][:, a * LANES:(a + 1) * LANES] - m_news[x]) for a in range(nt)], axis=1)
              for x in chains]
        alphas = [jnp.exp(m_olds[x] - m_news[x]) for x in chains]
        pvs = [jnp.dot(ps[x].astype(BF16), v1s[x // 2], preferred_element_type=F32) for x in chains]
        for x in chains:
            acc_scr[x] = jnp.concatenate([alphas[x], alphas[x]], axis=1) * acc_scr[x] + pvs[x]
            m_scr[x] = m_news[x]

    @pl.when(j < i)
    def _():
        update(False)

    @pl.when(j == i)
    def _():
        update(True)
        for hh in range(ATTN_HEADS):
            o1 = acc_scr[2 * hh, :, 0:LANES] / acc_scr[2 * hh, :, LANES:]
            o2 = acc_scr[2 * hh + 1, :, 0:LANES] / acc_scr[2 * hh + 1, :, LANES:]
            out_ref[:, hh * LANES:(hh + 1) * LANES] = _da_finish(
                o1, o2, lam_ref[0:1, 0:1], nw_ref[...], post).astype(out_ref.dtype)


def _prompt_attention(q, k, v, lam_t, nw, nb, t, post):
    tq = _row_tile(t, 512)
    nq = t // tq
    pairs = [(i, j) for i in range(nq) for j in range(i + 1)]
    qi = jnp.asarray([p[0] for p in pairs], I32)
    kj = jnp.asarray([p[1] for p in pairs], I32)
    hw = ATTN_HEADS * LANES
    q_spec = pl.BlockSpec((tq, hw), lambda b, h, s, qi, kj: (b * nq + qi[s], h))
    kv_spec = pl.BlockSpec((tq, hw), lambda b, h, s, qi, kj: (b * nq + kj[s], h))
    const = pl.BlockSpec((1, LANES), lambda b, h, s, qi, kj: (0, 0))
    grid_spec = pltpu.PrefetchScalarGridSpec(
        num_scalar_prefetch=2,
        grid=(nb, DA_HEADS // ATTN_HEADS, len(pairs)),
        in_specs=[const, q_spec, kv_spec, kv_spec, const],
        out_specs=q_spec,
        scratch_shapes=[pltpu.VMEM((2 * ATTN_HEADS, tq, LANES), F32), pltpu.VMEM((2 * ATTN_HEADS, tq, 2 * LANES), F32)],
    )
    return pl.pallas_call(
        functools.partial(_attn_kernel, tq, post),
        grid_spec=grid_spec,
        out_shape=jax.ShapeDtypeStruct((nb * t, DA_W), BF16),
        compiler_params=_cparams(("parallel", "parallel", "arbitrary"), 48 << 20),
        name="prompt_attention",
    )(qi, kj, lam_t, q, k, v, nw)


def _paged_kernel(s_new, post, npg, pt_ref, lam_ref, qw_ref, *refs):
    k_refs, v_refs = refs[:npg], refs[npg:2 * npg]
    kn_ref, vn_ref, nw_ref, out_ref, m_scr, l_scr, acc_scr = refs[2 * npg:]
    p_id = pl.program_id(1)
    nrow = DA_HEADS * 2 * s_new
    qw = qw_ref[0]

    @pl.when(p_id == 0)
    def _():
        m_scr[...] = jnp.full_like(m_scr, NEG)
        l_scr[...] = jnp.zeros_like(l_scr)
        acc_scr[...] = jnp.zeros_like(acc_scr)

    def keep_mask(nk, causal):
        row = lax.broadcasted_iota(I32, (nrow, nk), 0)
        col = lax.broadcasted_iota(I32, (nrow, nk), 1)
        keep = (col & (DA_HEADS - 1)) == (row >> ((2 * s_new).bit_length() - 1))
        if causal:
            keep = keep & ((col >> (DA_HEADS.bit_length() - 1)) <= (row & (s_new - 1)))
        return keep

    def accumulate(kblks, vblks, keep):
        ss = [jnp.where(keep, _dot_nt(qw, kb.astype(BF16), prec=None), NEG) for kb in kblks]
        smax = ss[0]
        for s in ss[1:]:
            smax = jnp.maximum(smax, s)
        m_old = m_scr[...]
        m_new = jnp.maximum(m_old, jnp.max(smax, axis=-1, keepdims=True))
        ps = [jnp.where(keep, jnp.exp(s - m_new), 0.0) for s in ss]
        psum = ps[0]
        for p in ps[1:]:
            psum = psum + p
        alpha = jnp.exp(m_old - m_new)
        l_scr[...] = alpha * l_scr[...] + jnp.sum(psum, axis=-1, keepdims=True)
        pv = jnp.dot(ps[0].astype(BF16), vblks[0].astype(BF16), preferred_element_type=F32)
        for p, vb in zip(ps[1:], vblks[1:]):
            pv = pv + jnp.dot(p.astype(BF16), vb.astype(BF16), preferred_element_type=F32)
        acc_scr[...] = alpha * acc_scr[...] + pv
        m_scr[...] = m_new

    accumulate([r[0] for r in k_refs], [r[0] for r in v_refs], keep_mask(k_refs[0].shape[1], False))

    @pl.when(p_id == pl.num_programs(1) - 1)
    def _():
        accumulate([kn_ref[0]], [vn_ref[0]], keep_mask(kn_ref.shape[1], True))
        o = acc_scr[...] / l_scr[...]
        lam = lam_ref[0:1, 0:1]
        for h in range(DA_HEADS):
            o1 = o[h * 2 * s_new:h * 2 * s_new + s_new]
            o2 = o[h * 2 * s_new + s_new:(h + 1) * 2 * s_new]
            out_ref[0, h * s_new:(h + 1) * s_new, :] = _da_finish(o1, o2, lam, nw_ref[...], post)


def _paged_attention(page_table, lam_t, qw, cache_k, cache_v, k_new, v_new, nw, s_new, post):
    nb, n_pages = page_table.shape
    page_rows = cache_k.shape[1]
    nrow = DA_HEADS * 2 * s_new
    new_rows = k_new.shape[1]
    npg = math.gcd(n_pages, PAGES_PER_STEP)

    def page_spec(i):
        return pl.BlockSpec((1, page_rows, LANES), lambda b, p, pt: (pt[b, p * npg + i], 0, 0))

    grid_spec = pltpu.PrefetchScalarGridSpec(
        num_scalar_prefetch=1,
        grid=(nb, n_pages // npg),
        in_specs=[pl.BlockSpec((1, LANES), lambda b, p, pt: (0, 0)),
                  pl.BlockSpec((1, nrow, LANES), lambda b, p, pt: (b, 0, 0))]
                 + [page_spec(i) for i in range(npg)] * 2
                 + [pl.BlockSpec((1, new_rows, LANES), lambda b, p, pt: (b, 0, 0)),
                    pl.BlockSpec((1, new_rows, LANES), lambda b, p, pt: (b, 0, 0)),
                    pl.BlockSpec((1, LANES), lambda b, p, pt: (0, 0))],
        out_specs=pl.BlockSpec((1, DA_HEADS * s_new, LANES), lambda b, p, pt: (b, 0, 0)),
        scratch_shapes=[pltpu.VMEM((nrow, 1), F32), pltpu.VMEM((nrow, 1), F32), pltpu.VMEM((nrow, LANES), F32)],
    )
    return pl.pallas_call(
        functools.partial(_paged_kernel, s_new, post, npg),
        grid_spec=grid_spec,
        out_shape=jax.ShapeDtypeStruct((nb, DA_HEADS * s_new, LANES), F32),
        compiler_params=_cparams(("parallel", "arbitrary"), 32 << 20),
        name="paged_attention",
    )(page_table, lam_t, qw, *([cache_k] * npg), *([cache_v] * npg), k_new, v_new, nw)


def _layer_norm(y, g, b):
    mu = jnp.mean(y, axis=-1, keepdims=True)
    d = y - mu
    var = jnp.mean(d * d, axis=-1, keepdims=True)
    return d * lax.rsqrt(var + 1e-5) * g + b


def _mix_kernel(alpha, x_ref, dn_ref, da_ref, w_ref, g_ref, b_ref, x1_ref, x1t_ref):
    mix = (jnp.dot(dn_ref[...], w_ref[0:DN_QK, :], preferred_element_type=F32)
           + jnp.dot(da_ref[...], w_ref[DN_QK:, :], preferred_element_type=F32))
    x1 = _layer_norm(alpha * x_ref[...] + mix, g_ref[...], b_ref[...])
    x1_ref[...] = x1
    for s in range(SUBLANES):
        x1t_ref[:, s, :] = x1[:, s * LANES:(s + 1) * LANES]


def _mix(x2d, dn, da, w_out, g, b, alpha):
    m = x2d.shape[0]
    tm = _row_tile(m, 256)
    row = lambda w: pl.BlockSpec((tm, w), lambda i: (i, 0))
    full = lambda shape: pl.BlockSpec(shape, lambda i: (0,) * len(shape))
    return pl.pallas_call(
        functools.partial(_mix_kernel, alpha),
        grid=(m // tm,),
        in_specs=[row(D_MODEL), row(DN_QK), row(DA_W), full((D_MODEL, D_MODEL)), full((1, D_MODEL)), full((1, D_MODEL))],
        out_specs=[row(D_MODEL), pl.BlockSpec((tm, SUBLANES, LANES), lambda i: (i, 0, 0))],
        out_shape=[jax.ShapeDtypeStruct((m, D_MODEL), F32), jax.ShapeDtypeStruct((m, SUBLANES, LANES), F32)],
        compiler_params=_cparams(("parallel",), 32 << 20),
        name="mix_ln1",
    )(x2d, dn, da, w_out, g, b)


def _top16_rows(vals, payload=None):
    r = vals.shape[0]
    iota = lax.broadcasted_iota(I32, vals.shape, 0)
    tops, idxs = [], []
    for _ in range(PEER_TOPK):
        m = jnp.max(vals, axis=0, keepdims=True)
        am = jnp.min(jnp.where(vals == m, iota, r), axis=0, keepdims=True)
        hit = iota == am
        tops.append(m)
        idxs.append(am if payload is None else jnp.sum(jnp.where(hit, payload, 0), axis=0, keepdims=True))
        vals = jnp.where(hit, -jnp.inf, vals)
    return tops, idxs


def _route_kernel(x_ref, wq_ref, sk_ref, code_ref, gate_ref):
    xb = x_ref[...].astype(BF16)
    half = PEER_DK // 2
    gates, codes = [], []
    for h in range(PEER_HEADS):
        picks = []
        for c in range(2):
            col = (h * 2 + c) * half
            q = jnp.dot(xb, wq_ref[:, col:col + half], preferred_element_type=F32)
            st = _dot_nt(sk_ref[c], q.astype(BF16), prec=None)
            picks.append(_top16_rows(st))
        (s1, i1), (s2, i2) = picks
        s2m = jnp.concatenate(s2, axis=0)
        i2m = jnp.concatenate(i2, axis=0)
        sub8 = lax.broadcasted_iota(I32, (SUBLANES, s2m.shape[1]), 0)
        cand = [s1[0] + s2m]
        cidx = [i1[0] * N_KEYS + i2m]
        for a in range(1, SUBLANES):
            cand.append(jnp.where(sub8 < PEER_TOPK // (a + 1), s1[a] + s2m[:SUBLANES], -jnp.inf))
            cidx.append(i1[a] * N_KEYS + i2m[:SUBLANES])
        cand.append(jnp.concatenate(s1[SUBLANES:], axis=0) + s2[0])
        cidx.append(jnp.concatenate(i1[SUBLANES:], axis=0) * N_KEYS + i2[0])
        top, eidx = _top16_rows(jnp.concatenate(cand, axis=0), jnp.concatenate(cidx, axis=0))
        ex = [jnp.exp(t - top[0]) for t in top]
        den = ex[0]
        for e in ex[1:]:
            den = den + e
        gates.append(jnp.concatenate(ex, axis=0) / den)
        e = jnp.concatenate(eidx, axis=0)
        codes.append(((e & (HALF_EXPERTS - 1)) << (U_ROW_SHIFT + 3)) | (16 - ((e >> 13) << 4)))
    gate_ref[...] = jnp.concatenate(gates, axis=0).T
    code_ref[...] = jnp.concatenate(codes, axis=0).T


def _route(x1, wq, sk):
    m = x1.shape[0]
    tm = _row_tile(m, 256)
    full = lambda shape: pl.BlockSpec(shape, lambda i: (0,) * len(shape))
    row = pl.BlockSpec((tm, N_PAIRS), lambda i: (i, 0))
    return pl.pallas_call(
        _route_kernel,
        grid=(m // tm,),
        in_specs=[pl.BlockSpec((tm, D_MODEL), lambda i: (i, 0)), full(wq.shape), full(sk.shape)],
        out_specs=[row, row],
        out_shape=[jax.ShapeDtypeStruct((m, N_PAIRS), I32), jax.ShapeDtypeStruct((m, N_PAIRS), F32)],
        compiler_params=_cparams(("parallel",), 32 << 20),
        name="peer_route",
    )(x1, wq, sk)


def _expert_row(tab_ref, row, shift_v):
    word = tab_ref[pl.ds(pl.multiple_of(row, SUBLANES), SUBLANES), :]
    return pltpu.bitcast((word << shift_v) & HI_HALF, F32)


def _splat(s):
    return jnp.full((SUBLANES, LANES), s, I32)


def _fold_pair(a, b, s, sub):
    m = (sub & s) == 0
    return jnp.where(m, a, b) + pltpu.roll(jnp.where(m, b, a), s, axis=0)


def _sublane_sums(p, sub):
    c = [_fold_pair(p[2 * i], p[2 * i + 1], 1, sub) for i in range(4)]
    d = [_fold_pair(c[0], c[1], 2, sub), _fold_pair(c[2], c[3], 2, sub)]
    return _fold_pair(d[0], d[1], 4, sub)


def _peer_u_kernel(tb, rt, code_ref, x_ref, gate_ref, codev_ref, tab_ref, wc_ref, part_scr, h_scr):
    sub = lax.broadcasted_iota(I32, (SUBLANES, LANES), 0)
    ones = jnp.ones((SUBLANES, LANES), BF16)

    def batch(o, carry):
        def token(tt, carry):
            t = o * rt + tt
            xv = x_ref[t]
            for g in range(N_PAIRS // SUBLANES):
                codes = [code_ref[t * N_PAIRS + g * SUBLANES + i] for i in range(SUBLANES)]
                prods = [_expert_row(tab_ref, lax.shift_right_logical(c, U_ROW_SHIFT), _splat(c) & 31) * xv
                         for c in codes]
                row0 = pl.multiple_of(tt * N_PAIRS + g * SUBLANES, SUBLANES)
                part_scr[pl.ds(row0, SUBLANES), :] = _sublane_sums(prods, sub)
            return carry

        lax.fori_loop(0, rt, token, 0)
        hi, lo = _split_bf16(part_scr[...])
        tot = (lax.dot_general(ones, hi, (((1,), (1,)), ((), ())), preferred_element_type=F32)
               + lax.dot_general(ones, lo, (((1,), (1,)), ((), ())), preferred_element_type=F32))
        for q in range(rt // SUBLANES):
            tile = tot[:, q * SUBLANES * N_PAIRS:(q * SUBLANES + 1) * N_PAIRS]
            for j in range(1, SUBLANES):
                tile = jnp.where(sub == j, tot[:, (q * SUBLANES + j) * N_PAIRS:(q * SUBLANES + j + 1) * N_PAIRS], tile)
            h_scr[pl.ds(pl.multiple_of(o * rt + q * SUBLANES, SUBLANES), SUBLANES), :] = tile
        return carry

    lax.fori_loop(0, tb // rt, batch, 0)
    h = h_scr[...]
    w = gate_ref[...] * (0.5 * h * (1.0 + lax.erf(h * (0.5 ** 0.5))))
    cv = codev_ref[...]
    code_v = lax.shift_right_logical(cv, U_ROW_SHIFT) | ((cv >> 4) & 1)
    wc_ref[...] = pltpu.bitcast(w.astype(BF16).astype(F32), I32) | code_v


def _peer_v_kernel(tb, wc_ref, tab_ref, out_ref):
    nacc = 4

    def token(t, carry):
        accs = [jnp.zeros((SUBLANES, LANES), F32) for _ in range(nacc)]
        for k in range(N_PAIRS):
            wc = wc_ref[t * N_PAIRS + k]
            wc_v = _splat(wc)
            weight = pltpu.bitcast(wc_v & HI_HALF, F32)
            row = _expert_row(tab_ref, wc & ROW_MASK, (wc_v & 1) << 4)
            accs[k % nacc] = accs[k % nacc] + row * weight
        out_ref[t] = (accs[0] + accs[1]) + (accs[2] + accs[3])
        return carry

    lax.fori_loop(0, tb, token, 0)


def _peer_tb(m):
    return LANES if m % LANES == 0 else m


def _peer_u(code, x1t, gate, tab):
    m = x1t.shape[0]
    tb = _peer_tb(m)
    rt = math.gcd(tb, LANE_SUM_TOKENS)
    row = pl.BlockSpec((tb, N_PAIRS), lambda i: (i, 0))
    return pl.pallas_call(
        functools.partial(_peer_u_kernel, tb, rt),
        grid=(m // tb,),
        in_specs=[pl.BlockSpec((tb * N_PAIRS,), lambda i: (i,), memory_space=pltpu.SMEM),
                  pl.BlockSpec((tb, SUBLANES, LANES), lambda i: (i, 0, 0)), row, row,
                  pl.BlockSpec(tab.shape, lambda i: (0, 0))],
        out_specs=row,
        out_shape=jax.ShapeDtypeStruct((m, N_PAIRS), I32),
        scratch_shapes=[pltpu.VMEM((rt * N_PAIRS, LANES), F32), pltpu.VMEM((tb, N_PAIRS), F32)],
        compiler_params=_cparams(("arbitrary",), VMEM_PHYSICAL - (8 << 20)),
        name="peer_u",
    )(code.reshape(-1), x1t, gate, code, tab)


def _peer_v(wc, tab):
    m = wc.shape[0]
    tb = _peer_tb(m)
    return pl.pallas_call(
        functools.partial(_peer_v_kernel, tb),
        grid=(m // tb,),
        in_specs=[pl.BlockSpec((tb * N_PAIRS,), lambda i: (i,), memory_space=pltpu.SMEM),
                  pl.BlockSpec(tab.shape, lambda i: (0, 0))],
        out_specs=pl.BlockSpec((tb, SUBLANES, LANES), lambda i: (i, 0, 0)),
        out_shape=jax.ShapeDtypeStruct((m, SUBLANES, LANES), F32),
        compiler_params=_cparams(("arbitrary",), VMEM_PHYSICAL - (8 << 20)),
        name="peer_v",
    )(wc.reshape(-1), tab)


def _ln2_kernel(alpha, x1_ref, ffn_ref, g_ref, b_ref, y_ref):
    ffn = jnp.concatenate([ffn_ref[:, s, :] for s in range(SUBLANES)], axis=1)
    y_ref[...] = _layer_norm(alpha * x1_ref[...] + ffn, g_ref[...], b_ref[...])


def _ln2(x1, ffn_t, g, b, alpha):
    m = x1.shape[0]
    tm = _row_tile(m, 256)
    row = pl.BlockSpec((tm, D_MODEL), lambda i: (i, 0))
    full = pl.BlockSpec((1, D_MODEL), lambda i: (0, 0))
    return pl.pallas_call(
        functools.partial(_ln2_kernel, alpha),
        grid=(m // tm,),
        in_specs=[row, pl.BlockSpec((tm, SUBLANES, LANES), lambda i: (i, 0, 0)), full, full],
        out_specs=row,
        out_shape=jax.ShapeDtypeStruct((m, D_MODEL), F32),
        compiler_params=_cparams(("parallel",), 32 << 20),
        name="ln2",
    )(x1, ffn_t, g, b)


def _pack_table(tab):
    bits = lax.bitcast_convert_type(tab.astype(BF16), jnp.uint16).astype(jnp.uint32)
    word = bits[:HALF_EXPERTS] | (bits[HALF_EXPERTS:] << 16)
    return lax.bitcast_convert_type(word, I32).reshape(HALF_EXPERTS * SUBLANES, LANES)


def _rope_tables(pos):
    half = DA_DQK // 2
    inv_freq = ROPE_THETA ** (-jnp.arange(half, dtype=F32) * 2.0 / DA_DQK)
    ang = pos.astype(F32)[:, None] * inv_freq[None, :]
    cos, sin = jnp.cos(ang), jnp.sin(ang)
    return jnp.tile(cos, (1, LANES // half)), jnp.tile(jnp.concatenate([-sin, sin], axis=1), (1, LANES // DA_DQK))


def _peer_and_norm(x1, x1t, p):
    code, gate = _route(x1, p["wq"], p["sk"])
    wc = _peer_u(code, x1t, gate, p["u_tab"])
    ffn_t = _peer_v(wc, p["v_tab"])
    return _ln2(x1, ffn_t, p["ln2_g"], p["ln2_b"], p["alpha"])


def kernel(x_prompt, x_sample, cache_k, cache_v, state_dn, state_conv, page_table, w_in, dn_conv_w, dn_a_log,
           dn_dt_bias, dn_norm_w, da_lambda_q1, da_lambda_k1, da_lambda_q2, da_lambda_k2, da_norm_w, w_out,
           ln1_g, ln1_b, peer_wq, peer_sub_keys, peer_u, peer_v, ln2_g, ln2_b):
    depth = w_in.shape[0]
    assert depth == 1, "single-layer trunk"
    nb, t, _ = x_prompt.shape
    nsb, s_new, _ = x_sample.shape
    assert s_new & (s_new - 1) == 0, "sample length must be a power of two"
    n_pages, page = page_table.shape[1], cache_k.shape[2]
    past_len = n_pages * page
    alpha = (2 * depth) ** 0.25
    lam_init = 0.8 - 0.6 * math.exp(-0.3 * 0)
    post = 1.0 - lam_init
    lam = (jnp.exp(jnp.sum(da_lambda_q1[0] * da_lambda_k1[0])) - jnp.exp(jnp.sum(da_lambda_q2[0] * da_lambda_k2[0]))
           + lam_init)
    lam_t = jnp.full((1, LANES), lam, F32)

    o_ab = DN_CH + DN_QK
    w0 = w_in[0]
    w_r = jnp.concatenate([w0[:, :o_ab], w0[:, o_ab + 2 * DN_HEADS:], w0[:, o_ab:o_ab + 2 * DN_HEADS],
                           jnp.zeros((D_MODEL, LANES - 2 * DN_HEADS), F32)], axis=1).astype(BF16)
    hp = jnp.zeros((SUBLANES, LANES), F32).at[0, :DN_HEADS].set(dn_a_log[0]).at[1, :DN_HEADS].set(dn_dt_bias[0])
    dn_nw = dn_norm_w[0].reshape(1, DN_D)
    da_nw = da_norm_w[0].reshape(1, DA_DV)
    p = dict(wq=peer_wq[0].astype(BF16), sk=peer_sub_keys[0].astype(BF16), u_tab=_pack_table(peer_u[0]),
             v_tab=_pack_table(peer_v[0]), ln2_g=ln2_g[0].reshape(1, D_MODEL), ln2_b=ln2_b[0].reshape(1, D_MODEL),
             alpha=alpha)
    w_out_b = w_out[0].astype(BF16)
    g1, b1 = ln1_g[0].reshape(1, D_MODEL), ln1_b[0].reshape(1, D_MODEL)

    xp = x_prompt.reshape(nb * t, D_MODEL)
    cos_p, sin_p = _rope_tables(jnp.tile(jnp.arange(t), nb))
    qkv, z, ab, q, kf, kb, vf, vb = _proj(xp, w_r, cos_p, sin_p)
    c = min(DN_CHUNK, t)
    seq = lambda a: a.reshape(nb, t, a.shape[-1])
    dn, s_p, c_p = _deltanet(seq(qkv), seq(z), seq(ab), dn_conv_w[0], hp, dn_nw, c, c)
    da = _prompt_attention(q, kb, vb, lam_t, da_nw, nb, t, post)
    x1, x1t = _mix(xp, dn.reshape(nb * t, DN_QK), da, w_out_b, g1, b1, alpha)
    y_p = _peer_and_norm(x1, x1t, p).reshape(nb, t, D_MODEL)
    k_p = kf.reshape(1, nb, t, DA_HEADS, 2 * DA_DQK)
    v_p = vf.reshape(1, nb, t, DA_HEADS, DA_DV)

    xs = x_sample.reshape(nsb * s_new, D_MODEL)
    cos_s, sin_s = _rope_tables(jnp.tile(past_len + jnp.arange(s_new), nsb))
    qkv, z, ab, q, kf, kb, vf, vb = _proj(xs, w_r, cos_s, sin_s)
    cs = SUBLANES * (-(-s_new // SUBLANES))
    pad = lambda a: jnp.pad(a.reshape(nsb, s_new, -1), ((0, 0), (0, cs - s_new), (0, 0)))
    dn, s_s, c_s = _deltanet(pad(qkv), pad(z), pad(ab), dn_conv_w[0], hp, dn_nw, cs, s_new,
                             s0=state_dn[0], cb0=state_conv[0])
    dn = dn[:, :s_new].reshape(nsb * s_new, DN_QK)
    q5 = q.reshape(nsb, s_new, DA_HEADS, 2, DA_DQK).transpose(0, 2, 3, 1, 4)
    qw = (q5[:, :, :, :, None, :] * jnp.eye(2, dtype=BF16)[None, None, :, None, :, None]).reshape(
        nsb, DA_HEADS * 2 * s_new, 2 * DA_DQK)
    new_rows = s_new * DA_HEADS
    new_pad = SUBLANES * 2 * (-(-new_rows // (SUBLANES * 2)))
    padn = lambda a: jnp.pad(a.reshape(nsb, new_rows, LANES), ((0, 0), (0, new_pad - new_rows), (0, 0)))
    da = _paged_attention(page_table, lam_t, qw, cache_k[0].reshape(-1, page * DA_HEADS, LANES),
                          cache_v[0].reshape(-1, page * DA_HEADS, LANES), padn(kf), padn(vf), da_nw, s_new, post)
    da = da.reshape(nsb, DA_HEADS, s_new, DA_DV).transpose(0, 2, 1, 3).reshape(nsb * s_new, DA_W).astype(BF16)
    x1, x1t = _mix(xs, dn, da, w_out_b, g1, b1, alpha)
    y_s = _peer_and_norm(x1, x1t, p).reshape(nsb, s_new, D_MODEL)
    k_s = kf.reshape(1, nsb, s_new, DA_HEADS, 2 * DA_DQK)
    v_s = vf.reshape(1, nsb, s_new, DA_HEADS, DA_DV)

    return (y_p, y_s, k_p, v_p, k_s, v_s, s_p[None], c_p[None], s_s[None], c_s[None])
```
